```python
import functools
import jax, jax.numpy as jnp
from jax import lax
import numpy as np

D_MODEL = 1024
BATCH = 2
SEQ = 8192
DEPTH = 1
DEC_BATCH = 128
DEC_SEQ = 4
PAST_LEN = 2048
PAGE_SIZE = 128

N_HEADS = 8
HEAD_DIM = 64
D_ATTN = N_HEADS * HEAD_DIM
Q_BLOCK = 128
FORGET_BIAS = 3.0
D_CONV = D_MODEL // 2
CONV_W = 3
N_EXPERTS = 32
TOP_K = 4
D_EXPERT = D_MODEL
SWIGLU_LIMIT = 7.0
SWIGLU_ALPHA = 1.702
MOE_BLOCK = 128
D_PLE = 256
LN_EPS = 1e-5
DEEPNORM_ALPHA = (2 * DEPTH) ** 0.25
DEEPNORM_BETA = (8 * DEPTH) ** -0.25
D_IN = 3 * D_CONV + 3 * D_ATTN + N_HEADS + 2 * D_MODEL

kernel_name = 'hybrid_conv_fox_moe_decoder_step'


def mixer_split_sizes():
    return [D_CONV, D_CONV, D_CONV, D_ATTN, D_ATTN, D_ATTN, N_HEADS, D_MODEL, D_MODEL]


def layer_norm(x, g, b):
    xf = x.astype(jnp.float32)
    mu = jnp.mean(xf, axis=-1, keepdims=True)
    var = jnp.mean(jnp.square(xf - mu), axis=-1, keepdims=True)
    y = (xf - mu) * lax.rsqrt(var + LN_EPS) * g.astype(jnp.float32) + b.astype(jnp.float32)
    return y.astype(x.dtype)


def short_conv(u, prev, conv_w):
    seq = u.shape[1]
    ext = jnp.concatenate([prev.astype(u.dtype), u], axis=1)
    y = ext[:, 0:seq] * conv_w[0]
    for j in range(1, CONV_W):
        y = y + ext[:, j:j + seq] * conv_w[j]
    return y, ext[:, seq:]


def prompt_attention(q, k, v, logf):
    b, s = q.shape[0], q.shape[1]
    nb = s // Q_BLOCK
    scale = HEAD_DIM ** -0.5
    c = jnp.cumsum(logf, axis=1)
    ck = jnp.transpose(c, (0, 2, 1))[:, :, None, :]
    kpos = jnp.arange(s)
    qb = jnp.transpose(q.reshape(b, nb, Q_BLOCK, N_HEADS, HEAD_DIM), (1, 0, 2, 3, 4))
    cqb = jnp.transpose(c.reshape(b, nb, Q_BLOCK, N_HEADS), (1, 0, 3, 2))

    def one_block(args):
        qi, cqi, i = args
        sc = jnp.einsum('bqhd,bkhd->bhqk', qi, k, preferred_element_type=jnp.float32) * scale
        sc = sc + (cqi[..., None] - ck)
        qpos = i * Q_BLOCK + jnp.arange(Q_BLOCK)
        sc = jnp.where(kpos[None, :] <= qpos[:, None], sc, -jnp.inf)
        p = jax.nn.softmax(sc, axis=-1)
        return jnp.einsum('bhqk,bkhd->bqhd', p.astype(v.dtype), v)

    out = lax.map(one_block, (qb, cqb, jnp.arange(nb)))
    return jnp.transpose(out, (1, 0, 2, 3, 4)).reshape(b, s, D_ATTN)


def sample_attention(q, k, v, logf, k_past, v_past, logf_past):
    n, t = q.shape[0], q.shape[1]
    past = k_past.shape[1]
    scale = HEAD_DIM ** -0.5
    k_all = jnp.concatenate([k_past.astype(k.dtype), k], axis=1)
    v_all = jnp.concatenate([v_past.astype(v.dtype), v], axis=1)
    c = jnp.cumsum(jnp.concatenate([logf_past.astype(jnp.float32), logf], axis=1), axis=1)
    cq = jnp.transpose(c[:, past:], (0, 2, 1))[..., None]
    ck = jnp.transpose(c, (0, 2, 1))[:, :, None, :]
    sc = jnp.einsum('bqhd,bkhd->bhqk', q, k_all, preferred_element_type=jnp.float32) * scale
    sc = sc + (cq - ck)
    qpos = past + jnp.arange(t)
    kpos = jnp.arange(past + t)
    sc = jnp.where(kpos[None, :] <= qpos[:, None], sc, -jnp.inf)
    p = jax.nn.softmax(sc, axis=-1)
    out = jnp.einsum('bhqk,bkhd->bqhd', p.astype(v_all.dtype), v_all)
    return out.reshape(n, t, D_ATTN)


def token_mixer(x, conv_prev, attend, w_in, b_f, conv_w, w_br_conv, w_br_attn, w_o):
    n, s = x.shape[0], x.shape[1]
    z = jnp.einsum('nsd,de->nse', x, w_in)
    points = np.cumsum(mixer_split_sizes())[:-1].tolist()
    c_b, c_c, c_h, q, k, v, f_logit, g_conv, g_attn = jnp.split(z, points, axis=-1)
    y_conv, conv_new = short_conv(c_c * c_h, conv_prev, conv_w)
    y_conv = c_b * y_conv
    q = q.reshape(n, s, N_HEADS, HEAD_DIM)
    k = k.reshape(n, s, N_HEADS, HEAD_DIM)
    v = v.reshape(n, s, N_HEADS, HEAD_DIM)
    logf = jax.nn.log_sigmoid(f_logit.astype(jnp.float32) + b_f.astype(jnp.float32))
    y_attn = attend(q, k, v, logf)
    merged = (jax.nn.sigmoid(g_conv) * (y_conv @ w_br_conv)
              + jax.nn.sigmoid(g_attn) * (y_attn @ w_br_attn))
    return merged @ w_o, (k, v, logf, conv_new)


def moe(x, w_router, b_router, w_gate, b_gate, w_up, b_up, w_down, b_down):
    shape = x.shape
    xt = x.reshape(-1, shape[-1])
    n_tok = xt.shape[0]
    n_assign = n_tok * TOP_K
    n_blocks = -(-n_assign // MOE_BLOCK) + N_EXPERTS
    logits = (jnp.einsum('td,de->te', xt, w_router, preferred_element_type=jnp.float32)
              + b_router.astype(jnp.float32))
    top_val, top_idx = lax.top_k(logits, TOP_K)
    probs = jax.nn.softmax(top_val, axis=-1)
    flat_e = top_idx.reshape(-1)
    order = jnp.argsort(flat_e)
    e_sorted = flat_e[order]
    counts = jnp.bincount(flat_e, length=N_EXPERTS).astype(jnp.int32)
    start = jnp.cumsum(counts) - counts
    padded = (counts + MOE_BLOCK - 1) // MOE_BLOCK * MOE_BLOCK
    pad_end = jnp.cumsum(padded)
    pad_start = pad_end - padded
    dest = pad_start[e_sorted] + (jnp.arange(n_assign, dtype=jnp.int32) - start[e_sorted])
    buf = jnp.zeros((n_blocks * MOE_BLOCK, shape[-1]), xt.dtype).at[dest].set(xt[order // TOP_K])
    block_e = jnp.minimum(
        jnp.searchsorted(pad_end, jnp.arange(n_blocks, dtype=jnp.int32) * MOE_BLOCK, side='right'),
        N_EXPERTS - 1)

    def expert_block(args):
        xb, e = args
        g = xb @ w_gate[e] + b_gate[e]
        u = xb @ w_up[e] + b_up[e]
        g = jnp.minimum(g, SWIGLU_LIMIT)
        u = jnp.clip(u, -SWIGLU_LIMIT, SWIGLU_LIMIT)
        h = (u + 1.0) * (g * jax.nn.sigmoid(SWIGLU_ALPHA * g))
        return h @ w_down[e] + b_down[e]

    out = lax.map(expert_block, (buf.reshape(n_blocks, MOE_BLOCK, shape[-1]), block_e))
    out_sorted = out.reshape(-1, shape[-1])[dest]
    out_assign = jnp.zeros_like(out_sorted).at[order].set(out_sorted).reshape(n_tok, TOP_K, shape[-1])
    y = jnp.einsum('tkd,tk->td', out_assign, probs.astype(out_assign.dtype))
    return y.reshape(shape)


def setup_inputs(seed: int = 0) -> dict:
    key = jax.random.key(seed)
    ks = jax.random.split(key, 32)
    f32 = jnp.float32
    n_pages = PAST_LEN // PAGE_SIZE
    n_used = DEC_BATCH * n_pages
    n_pool = n_used + (n_used + 3) // 4
    beta = DEEPNORM_BETA

    def nrm(k, shape, scale=1.0):
        return jax.random.normal(k, shape, f32) * scale

    return dict(
        x_prompt=nrm(ks[0], (BATCH, SEQ, D_MODEL)),
        x_sample=nrm(ks[1], (DEC_BATCH, DEC_SEQ, D_MODEL)),
        cache_k=nrm(ks[2], (DEPTH, n_pool, PAGE_SIZE, N_HEADS, HEAD_DIM)),
        cache_v=nrm(ks[3], (DEPTH, n_pool, PAGE_SIZE, N_HEADS, HEAD_DIM)),
        cache_logf=jax.nn.log_sigmoid(FORGET_BIAS + nrm(ks[4], (DEPTH, n_pool, PAGE_SIZE, N_HEADS))),
        state_conv=nrm(ks[5], (DEPTH, DEC_BATCH, CONV_W - 1, D_CONV)),
        page_table=jax.random.permutation(ks[6], n_pool)[:n_used].reshape(DEC_BATCH, n_pages).astype(jnp.int32),
        p_prompt=nrm(ks[7], (DEPTH, BATCH, SEQ, D_PLE)),
        p_sample=nrm(ks[8], (DEPTH, DEC_BATCH, DEC_SEQ, D_PLE)),
        ln_in_g=1.0 + nrm(ks[9], (D_MODEL,), 0.02),
        ln_in_b=nrm(ks[10], (D_MODEL,), 0.02),
        w_in=nrm(ks[11], (DEPTH, D_MODEL, D_IN), D_MODEL ** -0.5),
        b_f=FORGET_BIAS + nrm(ks[12], (DEPTH, N_HEADS), 0.1),
        conv_w=nrm(ks[13], (DEPTH, CONV_W, D_CONV), CONV_W ** -0.5),
        w_br_conv=nrm(ks[14], (DEPTH, D_CONV, D_MODEL), D_CONV ** -0.5),
        w_br_attn=nrm(ks[15], (DEPTH, D_ATTN, D_MODEL), D_ATTN ** -0.5),
        w_o=nrm(ks[16], (DEPTH, D_MODEL, D_MODEL), beta * D_MODEL ** -0.5),
        ln1_g=1.0 + nrm(ks[17], (DEPTH, D_MODEL), 0.02),
        ln1_b=nrm(ks[18], (DEPTH, D_MODEL), 0.02),
        w_router=nrm(ks[19], (DEPTH, D_MODEL, N_EXPERTS), D_MODEL ** -0.5),
        b_router=nrm(ks[20], (DEPTH, N_EXPERTS), 0.01),
        w_gate=nrm(ks[21], (DEPTH, N_EXPERTS, D_MODEL, D_EXPERT), D_MODEL ** -0.5),
        b_gate=nrm(ks[22], (DEPTH, N_EXPERTS, D_EXPERT), 0.01),
        w_up=nrm(ks[23], (DEPTH, N_EXPERTS, D_MODEL, D_EXPERT), D_MODEL ** -0.5),
        b_up=nrm(ks[24], (DEPTH, N_EXPERTS, D_EXPERT), 0.01),
        w_down=nrm(ks[25], (DEPTH, N_EXPERTS, D_EXPERT, D_MODEL), beta * D_EXPERT ** -0.5),
        b_down=nrm(ks[26], (DEPTH, N_EXPERTS, D_MODEL), 0.01),
        ln2_g=1.0 + nrm(ks[27], (DEPTH, D_MODEL), 0.02),
        ln2_b=nrm(ks[28], (DEPTH, D_MODEL), 0.02),
        w_ple_gate=nrm(ks[29], (DEPTH, D_MODEL, D_MODEL), D_MODEL ** -0.5),
        w_ple_proj=nrm(ks[30], (DEPTH, D_PLE, D_MODEL), beta * D_PLE ** -0.5),
    )


def reference(x_prompt, x_sample, cache_k, cache_v, cache_logf, state_conv, page_table,
              p_prompt, p_sample, ln_in_g, ln_in_b, w_in, b_f, conv_w, w_br_conv, w_br_attn,
              w_o, ln1_g, ln1_b, w_router, b_router, w_gate, b_gate, w_up, b_up, w_down,
              b_down, ln2_g, ln2_b, w_ple_gate, w_ple_proj):
    n_seq = page_table.shape[0]
    past_len = page_table.shape[1] * cache_k.shape[2]

    def layer(x, p, conv_prev, attend, l):
        mix, new_state = token_mixer(x, conv_prev, attend, w_in[l], b_f[l], conv_w[l],
                                     w_br_conv[l], w_br_attn[l], w_o[l])
        x = layer_norm(DEEPNORM_ALPHA * x + mix, ln1_g[l], ln1_b[l])
        ffn = moe(x, w_router[l], b_router[l], w_gate[l], b_gate[l], w_up[l], b_up[l],
                  w_down[l], b_down[l])
        x = layer_norm(DEEPNORM_ALPHA * x + ffn, ln2_g[l], ln2_b[l])
        x = x + jax.nn.sigmoid(x @ w_ple_gate[l]) * (p.astype(x.dtype) @ w_ple_proj[l])
        return x, new_state

    xp = layer_norm(x_prompt, ln_in_g, ln_in_b)
    xs = layer_norm(x_sample, ln_in_g, ln_in_b)
    conv_zero = jnp.zeros((xp.shape[0], CONV_W - 1, D_CONV), xp.dtype)
    kp_l, vp_l, fp_l, cp_l = [], [], [], []
    ks_l, vs_l, fs_l, cs_l = [], [], [], []
    for l in range(DEPTH):
        xp, (kp, vp, fp, cp) = layer(xp, p_prompt[l], conv_zero, prompt_attention, l)
        k_past = cache_k[l][page_table].reshape(n_seq, past_len, N_HEADS, HEAD_DIM)
        v_past = cache_v[l][page_table].reshape(n_seq, past_len, N_HEADS, HEAD_DIM)
        f_past = cache_logf[l][page_table].reshape(n_seq, past_len, N_HEADS)
        attend = functools.partial(sample_attention, k_past=k_past, v_past=v_past, logf_past=f_past)
        xs, (k_s, v_s, f_s, c_s) = layer(xs, p_sample[l], state_conv[l], attend, l)
        kp_l.append(kp); vp_l.append(vp); fp_l.append(fp); cp_l.append(cp)
        ks_l.append(k_s); vs_l.append(v_s); fs_l.append(f_s); cs_l.append(c_s)

    return (xp, xs,
            jnp.stack(kp_l), jnp.stack(vp_l), jnp.stack(fp_l), jnp.stack(cp_l),
            jnp.stack(ks_l), jnp.stack(vs_l), jnp.stack(fs_l), jnp.stack(cs_l))
```

```python
import functools

import jax
import jax.numpy as jnp
from jax import lax
from jax.experimental import pallas as pl
from jax.experimental.pallas import tpu as pltpu

F32 = jnp.float32
BF16 = jnp.bfloat16
I32 = jnp.int32

N_HEADS = 8
HEAD_DIM = 64
D_ATTN = N_HEADS * HEAD_DIM
TOP_K = 4
CONV_TAPS = 3
N_EXPERTS = 32
LN_EPS = 1e-5
SWIGLU_LIMIT = 7.0
SWIGLU_ALPHA = 1.702
LANES = 128
SUBLANES = 8
VMEM_LIMIT = 56 * 1024 * 1024

NN = (((1,), (0,)), ((), ()))
NT = (((1,), (1,)), ((), ()))

ROW_TILE = 512
ATTN_TILE = 256
MOE_BLOCK = 256
MOE_TILE = 256


def _ln(x, g, b):
    mu = jnp.mean(x, axis=-1, keepdims=True)
    xc = x - mu
    var = jnp.mean(xc * xc, axis=-1, keepdims=True)
    return xc * lax.rsqrt(var + LN_EPS) * g + b


def _log_sigmoid(x):
    return jnp.minimum(x, 0.0) - jnp.log1p(jnp.exp(-jnp.abs(x)))


def _sigmoid(x):
    return 1.0 / (1.0 + jnp.exp(-x))


def _split3(a):
    hi = a.astype(BF16)
    r = a - hi.astype(F32)
    mid = r.astype(BF16)
    lo = (r - mid.astype(F32)).astype(BF16)
    return hi, mid, lo


def _dot_sel(a, sel, dims=NN):
    out = None
    for piece in _split3(a):
        t = lax.dot_general(piece, sel, dims, preferred_element_type=F32)
        out = t if out is None else out + t
    return out


def _sel_dot(sel, a):
    out = None
    for piece in _split3(a):
        t = lax.dot_general(sel, piece, NN, preferred_element_type=F32)
        out = t if out is None else out + t
    return out


def _qkv_kernel(x_ref, g_ref, b_ref, wqkv_ref, wft_ref, bf_ref, triu_ref,
                q_ref, k_ref, v_ref, kb_ref, vb_ref, lf_ref, c_ref, carry_ref, *, tiles_per_seq):
    i = pl.program_id(0)
    xb = _ln(x_ref[...], g_ref[...], b_ref[...]).astype(BF16)
    z = jnp.dot(xb, wqkv_ref[...], preferred_element_type=F32)
    q_ref[...] = z[:, :D_ATTN] * (HEAD_DIM ** -0.5)
    k = z[:, D_ATTN:2 * D_ATTN]
    v = z[:, 2 * D_ATTN:]
    k_ref[...] = k
    v_ref[...] = v
    kb_ref[...] = k.astype(BF16)
    vb_ref[...] = v.astype(BF16)
    ft = lax.dot_general(wft_ref[...], xb, NT, preferred_element_type=F32)[:N_HEADS]
    lf = _log_sigmoid(ft + bf_ref[...])
    lf_ref[...] = lf

    @pl.when(i % tiles_per_seq == 0)
    def _():
        carry_ref[...] = jnp.zeros_like(carry_ref)

    c = _dot_sel(lf, triu_ref[...]) + carry_ref[:, :1]
    c_ref[...] = c
    carry_ref[...] = jnp.broadcast_to(c[:, -1:], carry_ref.shape)


def _qkv(x, ln_g, ln_b, w_qkv, w_ft, b_f, seq):
    rows, d = x.shape
    tm = ROW_TILE
    triu = jnp.triu(jnp.ones((tm, tm), F32)).astype(BF16)
    row_spec = lambda w: pl.BlockSpec((tm, w), lambda i: (i, 0))
    t_spec = pl.BlockSpec((N_HEADS, tm), lambda i: (0, i))
    full = lambda a: pl.BlockSpec(a.shape, lambda i: (0,) * a.ndim)
    return pl.pallas_call(
        functools.partial(_qkv_kernel, tiles_per_seq=seq // tm),
        grid=(rows // tm,),
        in_specs=[row_spec(d), full(ln_g), full(ln_b), full(w_qkv), full(w_ft), full(b_f), full(triu)],
        out_specs=[row_spec(D_ATTN)] * 5 + [t_spec, t_spec],
        out_shape=[jax.ShapeDtypeStruct((rows, D_ATTN), F32)] * 3
        + [jax.ShapeDtypeStruct((rows, D_ATTN), BF16)] * 2
        + [jax.ShapeDtypeStruct((N_HEADS, rows), F32)] * 2,
        scratch_shapes=[pltpu.VMEM((N_HEADS, LANES), F32)],
        compiler_params=pltpu.CompilerParams(dimension_semantics=("arbitrary",),
                                             vmem_limit_bytes=VMEM_LIMIT),
        name="qkv",
    )(x, ln_g, ln_b, w_qkv, w_ft, b_f, triu)


def _pattn_kernel(q_ref, k_ref, v_ref, c_ref, o_ref, *, tile):
    hp = pl.program_id(1)
    qi = pl.program_id(2)
    q = q_ref[...].astype(BF16)
    lane = lax.broadcasted_iota(I32, (tile, LANES), 1)
    row = lax.broadcasted_iota(I32, (tile, tile), 0)
    col = lax.broadcasted_iota(I32, (tile, tile), 1)
    outs = []
    for hh in range(2):
        in_head = (lane < HEAD_DIM) if hh == 0 else (lane >= HEAD_DIM)
        qh = jnp.where(in_head, q, jnp.zeros_like(q))
        crow = 2 * hp + hh

        def step(kj, carry, diag, qh=qh, crow=crow):
            m, l, acc = carry
            ks = pl.multiple_of(kj * tile, tile)
            s = lax.dot_general(qh, k_ref[pl.ds(ks, tile), :], NT, preferred_element_type=F32)
            s = s - c_ref[pl.ds(crow, 1), pl.ds(ks, tile)]
            if diag:
                s = jnp.where(col <= row, s, -jnp.inf)
            m_new = jnp.maximum(m, jnp.max(s, axis=-1, keepdims=True))
            alpha = jnp.exp(m - m_new)
            p = jnp.exp(s - m_new)
            l = alpha * l + jnp.sum(p, axis=-1, keepdims=True)
            acc = alpha * acc + jnp.dot(p.astype(BF16), v_ref[pl.ds(ks, tile), :],
                                        preferred_element_type=F32)
            return m_new, l, acc

        init = (jnp.full((tile, 1), -jnp.inf, F32), jnp.zeros((tile, 1), F32),
                jnp.zeros((tile, LANES), F32))
        carry = lax.fori_loop(0, qi, functools.partial(step, diag=False), init)
        _, l, acc = step(qi, carry, True)
        outs.append(acc / l)
    o_ref[...] = jnp.where(lane < HEAD_DIM, outs[0], outs[1]).astype(BF16)


def _prompt_attention(q, kb, vb, c_t, batch, seq):
    tile = ATTN_TILE
    nq = seq // tile
    return pl.pallas_call(
        functools.partial(_pattn_kernel, tile=tile),
        grid=(batch, D_ATTN // LANES, nq),
        in_specs=[
            pl.BlockSpec((tile, LANES), lambda b, h, i: (b * nq + i, h)),
            pl.BlockSpec((seq, LANES), lambda b, h, i: (b, h)),
            pl.BlockSpec((seq, LANES), lambda b, h, i: (b, h)),
            pl.BlockSpec((N_HEADS, seq), lambda b, h, i: (0, b)),
        ],
        out_specs=pl.BlockSpec((tile, LANES), lambda b, h, i: (b * nq + i, h)),
        out_shape=jax.ShapeDtypeStruct((batch * seq, D_ATTN), BF16),
        compiler_params=pltpu.CompilerParams(
            dimension_semantics=("arbitrary", "arbitrary", "arbitrary"),
            vmem_limit_bytes=VMEM_LIMIT),
        name="pattn",
    )(q, kb, vb, c_t)


def _sattn_kernel(pt_ref, q_ref, kn_ref, vn_ref, lfn_ref, triu_ref, pagesel_ref, *rest,
                  n_pages, page, t_new):
    del pt_ref
    k_refs = rest[:n_pages]
    v_refs = rest[n_pages:2 * n_pages]
    f_refs = rest[2 * n_pages:3 * n_pages]
    o_ref = rest[3 * n_pages]
    nrow = t_new * N_HEADS
    q = q_ref[0]
    qrep = jnp.concatenate([jnp.broadcast_to(q[t:t + 1], (N_HEADS, D_ATTN)) for t in range(t_new)],
                           axis=0)
    rowi = lax.broadcasted_iota(I32, (nrow, D_ATTN), 0)
    lanei = lax.broadcasted_iota(I32, (nrow, D_ATTN), 1)
    head_mask = (lanei // HEAD_DIM) == (rowi % N_HEADS)
    qexp = jnp.where(head_mask, qrep, 0.0).astype(BF16)
    qexp_f = qexp.astype(F32)

    lf = jnp.concatenate([f_refs[p][0] for p in range(n_pages)], axis=0)
    c_in = _dot_sel(lf, triu_ref[...])
    tot = jnp.broadcast_to(c_in[:, page - 1:page], (n_pages * N_HEADS, page))
    c_all = c_in + _sel_dot(pagesel_ref[...], tot)
    c_tot = c_all[(n_pages - 1) * N_HEADS:, page - 1:page]

    logits = []
    for p in range(n_pages):
        kp = k_refs[p][0].astype(BF16)
        s = lax.dot_general(qexp, kp, NT, preferred_element_type=F32)
        cp = c_all[p * N_HEADS:(p + 1) * N_HEADS]
        logits.append(s - jnp.concatenate([cp] * t_new, axis=0))
    m = logits[0]
    for s in logits[1:]:
        m = jnp.maximum(m, s)
    m = jnp.max(m, axis=-1, keepdims=True)

    kn = kn_ref[0].astype(BF16).astype(F32)
    vn = vn_ref[0].astype(BF16).astype(F32)
    lfn = lfn_ref[0]
    trow = lax.broadcasted_iota(I32, (nrow, 1), 0) // N_HEADS
    new_logits = []
    cn = c_tot
    for j in range(t_new):
        cn = cn + lfn[:, j:j + 1]
        sj = jnp.sum(qexp_f * kn[j:j + 1], axis=-1, keepdims=True)
        sj = sj - jnp.concatenate([cn] * t_new, axis=0)
        sj = jnp.where(trow >= j, sj, -jnp.inf)
        new_logits.append(sj)
        m = jnp.maximum(m, sj)

    l = jnp.zeros((nrow, 1), F32)
    acc = jnp.zeros((nrow, D_ATTN), F32)
    for p in range(n_pages):
        pr = jnp.exp(logits[p] - m)
        l = l + jnp.sum(pr, axis=-1, keepdims=True)
        acc = acc + jnp.dot(pr.astype(BF16), v_refs[p][0].astype(BF16), preferred_element_type=F32)
    for j in range(t_new):
        pj = jnp.exp(new_logits[j] - m)
        l = l + pj
        acc = acc + pj.astype(BF16).astype(F32) * vn[j:j + 1]
    acc = jnp.where(head_mask, acc / l, 0.0)
    o_ref[0] = jnp.concatenate(
        [jnp.sum(acc[t * N_HEADS:(t + 1) * N_HEADS], axis=0, keepdims=True) for t in range(t_new)], axis=0)


def _sample_attention(page_table, q_s, k_s, v_s, lf_s, cache_k, cache_v, cache_lft):
    n_seq, n_pages = page_table.shape
    page = cache_k.shape[1]
    t_new = q_s.shape[1]
    triu = jnp.triu(jnp.ones((page, page), F32)).astype(BF16)
    r = jnp.arange(n_pages * N_HEADS)
    pagesel = ((r[:, None] % N_HEADS == r[None, :] % N_HEADS)
               & (r[None, :] // N_HEADS < r[:, None] // N_HEADS)).astype(BF16)

    def page_spec(p, shape):
        return pl.BlockSpec((1,) + shape, lambda n, pt, p=p: (pt[n * n_pages + p], 0, 0))

    seq_spec = lambda a: pl.BlockSpec((1,) + a.shape[1:], lambda n, pt: (n, 0, 0))
    const = lambda a: pl.BlockSpec(a.shape, lambda n, pt: (0, 0))
    grid_spec = pltpu.PrefetchScalarGridSpec(
        num_scalar_prefetch=1,
        grid=(n_seq,),
        in_specs=[seq_spec(q_s), seq_spec(k_s), seq_spec(v_s), seq_spec(lf_s), const(triu), const(pagesel)]
        + [page_spec(p, (page, D_ATTN)) for p in range(n_pages)]
        + [page_spec(p, (page, D_ATTN)) for p in range(n_pages)]
        + [page_spec(p, (N_HEADS, page)) for p in range(n_pages)],
        out_specs=pl.BlockSpec((1, t_new, D_ATTN), lambda n, pt: (n, 0, 0)),
    )
    return pl.pallas_call(
        functools.partial(_sattn_kernel, n_pages=n_pages, page=page, t_new=t_new),
        grid_spec=grid_spec,
        out_shape=jax.ShapeDtypeStruct((n_seq, t_new, D_ATTN), F32),
        compiler_params=pltpu.CompilerParams(dimension_semantics=("arbitrary",),
                                             vmem_limit_bytes=VMEM_LIMIT),
        name="sattn",
    )(page_table.reshape(-1), q_s, k_s, v_s, lf_s, triu, pagesel,
      *([cache_k] * n_pages), *([cache_v] * n_pages), *([cache_lft] * n_pages))


def _mixer_kernel(x_ref, yap_ref, yas_ref, tap0_ref, tap1_ref, lng_ref, lnb_ref, wc_ref, wg_ref,
                  cw_ref, wbc_ref, wba_ref, wo_ref, l1g_ref, l1b_ref, wr_ref, br_ref, tril_ref,
                  x1_ref, route_ref, cnt_ref, ulast_ref, us_ref,
                  ext_ref, cnt_acc_ref, *, tiles_per_seq, n_prompt_tiles, t_new, alpha):
    i = pl.program_id(0)
    tm = x_ref.shape[0]
    d_conv = us_ref.shape[1]
    is_sample = i >= n_prompt_tiles
    xn = _ln(x_ref[...], lng_ref[...], lnb_ref[...])
    xb = xn.astype(BF16)

    zc = jnp.dot(xb, wc_ref[...], preferred_element_type=F32)
    u = zc[:, d_conv:2 * d_conv] * zc[:, 2 * d_conv:]

    @pl.when(i % tiles_per_seq == 0)
    def _():
        ext_ref[0:SUBLANES, :] = jnp.zeros((SUBLANES, d_conv), F32)

    ext_ref[SUBLANES:, :] = u
    prev1 = ext_ref[SUBLANES - 1:SUBLANES - 1 + tm, :]
    prev2 = ext_ref[SUBLANES - 2:SUBLANES - 2 + tm, :]
    ext_ref[0:SUBLANES, :] = u[tm - SUBLANES:]
    t_in_seq = lax.broadcasted_iota(I32, (tm, 1), 0) % t_new
    keep1 = jnp.logical_or(jnp.logical_not(is_sample), t_in_seq >= 1)
    keep2 = jnp.logical_or(jnp.logical_not(is_sample), t_in_seq >= 2)
    prev1 = jnp.where(keep1, prev1, tap1_ref[...])
    prev2 = jnp.where(keep2, prev2, tap0_ref[...])
    cw = cw_ref[...]
    y_conv = zc[:, :d_conv] * (prev2 * cw[0:1] + prev1 * cw[1:2] + u * cw[2:3])
    ulast_ref[0] = u[tm - SUBLANES:]
    us_ref[...] = u

    ya = jnp.where(is_sample, yas_ref[...].astype(BF16), yap_ref[...])
    bc = jnp.dot(y_conv.astype(BF16), wbc_ref[...], preferred_element_type=F32)
    ba = jnp.dot(ya, wba_ref[...], preferred_element_type=F32)
    g = jnp.dot(xb, wg_ref[...], preferred_element_type=F32)
    d_model = bc.shape[1]
    merged = _sigmoid(g[:, :d_model]) * bc + _sigmoid(g[:, d_model:]) * ba
    mix = jnp.dot(merged.astype(BF16), wo_ref[...], preferred_element_type=F32)
    x1 = _ln(alpha * xn + mix, l1g_ref[...], l1b_ref[...])
    x1_ref[...] = x1

    logits = jnp.dot(x1, wr_ref[...], preferred_element_type=F32,
                     precision=lax.Precision.HIGHEST) + br_ref[...]
    lane = lax.broadcasted_iota(I32, (tm, LANES), 1)
    lane_f = lane.astype(F32)
    logits = jnp.where(lane < N_EXPERTS, logits, -jnp.inf)
    vals, hots = [], []
    for _ in range(TOP_K):
        mx = jnp.max(logits, axis=-1, keepdims=True)
        idx = jnp.min(jnp.where(logits == mx, lane_f, float(LANES)), axis=-1, keepdims=True)
        hot = lane_f == idx
        logits = jnp.where(hot, -jnp.inf, logits)
        vals.append(mx)
        hots.append(hot)
    exps = [jnp.exp(v - vals[0]) for v in vals]
    denom = exps[0]
    for e in exps[1:]:
        denom = denom + e

    @pl.when(i == 0)
    def _():
        cnt_acc_ref[...] = jnp.zeros_like(cnt_acc_ref)

    chosen = hots[0]
    for h in hots[1:]:
        chosen = jnp.logical_or(chosen, h)
    chosen_f = jnp.where(chosen, 1.0, 0.0)
    before = jnp.dot(tril_ref[...], chosen_f.astype(BF16), preferred_element_type=F32) + cnt_acc_ref[0:1, :]
    route = jnp.zeros((tm, LANES), F32)
    for kk in range(TOP_K):
        idx_f = jnp.sum(jnp.where(hots[kk], lane_f, 0.0), axis=-1, keepdims=True)
        rank = jnp.sum(jnp.where(hots[kk], before, 0.0), axis=-1, keepdims=True)
        route = jnp.where(lane == kk, idx_f, route)
        route = jnp.where(lane == TOP_K + kk, rank, route)
        route = jnp.where(lane == 2 * TOP_K + kk, exps[kk] / denom, route)
    route_ref[...] = route
    cnt = cnt_acc_ref[0:1, :] + jnp.sum(chosen_f, axis=0, keepdims=True)
    cnt_acc_ref[...] = jnp.broadcast_to(cnt, cnt_acc_ref.shape)
    cnt_ref[...] = jnp.broadcast_to(cnt, cnt_ref.shape)


def _mixer(x, ya_p, ya_s, tap0, tap1, ln_g, ln_b, w_c, w_g, conv_w, w_bc, w_ba, w_o, l1g, l1b,
           w_r, b_r, seq, n_prompt_rows, t_new, alpha):
    rows, d = x.shape
    tm = ROW_TILE
    n_prompt_tiles = n_prompt_rows // tm
    n_tiles = rows // tm
    d_conv = w_bc.shape[0]
    tril = jnp.tril(jnp.ones((tm, tm), F32), -1).astype(BF16)
    row_spec = lambda w: pl.BlockSpec((tm, w), lambda i: (i, 0))
    prompt_spec = lambda w: pl.BlockSpec((tm, w), lambda i: (jnp.minimum(i, n_prompt_tiles - 1), 0))
    sample_spec = lambda w: pl.BlockSpec((tm, w), lambda i: (jnp.maximum(i - n_prompt_tiles, 0), 0))
    full = lambda a: pl.BlockSpec(a.shape, lambda i: (0,) * a.ndim)
    weights = [ln_g, ln_b, w_c, w_g, conv_w, w_bc, w_ba, w_o, l1g, l1b, w_r, b_r, tril]
    n_sample_rows = rows - n_prompt_rows
    return pl.pallas_call(
        functools.partial(_mixer_kernel, tiles_per_seq=seq // tm, n_prompt_tiles=n_prompt_tiles,
                          t_new=t_new, alpha=alpha),
        grid=(n_tiles,),
        in_specs=[row_spec(d), prompt_spec(D_ATTN), sample_spec(D_ATTN), sample_spec(d_conv),
                  sample_spec(d_conv)] + [full(w) for w in weights],
        out_specs=[row_spec(d), row_spec(LANES), pl.BlockSpec((SUBLANES, LANES), lambda i: (0, 0)),
                   pl.BlockSpec((1, SUBLANES, d_conv), lambda i: (i, 0, 0)), sample_spec(d_conv)],
        out_shape=[jax.ShapeDtypeStruct((rows, d), F32), jax.ShapeDtypeStruct((rows, LANES), F32),
                   jax.ShapeDtypeStruct((SUBLANES, LANES), F32),
                   jax.ShapeDtypeStruct((n_tiles, SUBLANES, d_conv), F32),
                   jax.ShapeDtypeStruct((n_sample_rows, d_conv), F32)],
        scratch_shapes=[pltpu.VMEM((tm + SUBLANES, d_conv), F32), pltpu.VMEM((SUBLANES, LANES), F32)],
        compiler_params=pltpu.CompilerParams(dimension_semantics=("arbitrary",),
                                             vmem_limit_bytes=VMEM_LIMIT),
        name="mixer",
    )(x, ya_p, ya_s, tap0, tap1, *weights)


def _dispatch_kernel(meta_ref, dest_ref, x_ref, xs_ref, zero_ref, sem_ref, *, tile, block, n_blocks):
    i = pl.program_id(0)

    def row_copy(j, kk):
        return pltpu.make_async_copy(x_ref.at[pl.ds(j, 1), :],
                                     xs_ref.at[pl.ds(dest_ref[j * TOP_K + kk], 1), :], sem_ref.at[0])

    def zero_copy(start):
        return pltpu.make_async_copy(zero_ref, xs_ref.at[pl.ds(pl.multiple_of(start, block), block), :],
                                     sem_ref.at[1])

    def pad_copy(e):
        return zero_copy(meta_ref[e] - block)

    @pl.when(i == 0)
    def _():
        zero_ref[...] = jnp.zeros_like(zero_ref)
        n_used = meta_ref[2 * N_EXPERTS]
        for e in range(N_EXPERTS):
            @pl.when(meta_ref[N_EXPERTS + e] > 0)
            def _():
                pad_copy(e).start()

        def tail_start(b, carry):
            zero_copy(b * block).start()
            return carry

        def tail_wait(b, carry):
            zero_copy(b * block).wait()
            return carry

        lax.fori_loop(n_used, n_blocks, tail_start, 0)
        for e in range(N_EXPERTS):
            @pl.when(meta_ref[N_EXPERTS + e] > 0)
            def _():
                pad_copy(e).wait()
        lax.fori_loop(n_used, n_blocks, tail_wait, 0)

    def issue(j, carry):
        for kk in range(TOP_K):
            row_copy(j, kk).start()
        return carry

    def drain(j, carry):
        for kk in range(TOP_K):
            row_copy(j, kk).wait()
        return carry

    lax.fori_loop(0, tile, issue, 0)
    lax.fori_loop(0, tile, drain, 0)


def _dispatch(meta, dest, x1, n_rows_padded):
    rows, d = x1.shape
    tile = MOE_TILE
    grid_spec = pltpu.PrefetchScalarGridSpec(
        num_scalar_prefetch=1,
        grid=(rows // tile,),
        in_specs=[pl.BlockSpec((tile * TOP_K,), lambda i, m: (i,), memory_space=pltpu.SMEM),
                  pl.BlockSpec((tile, d), lambda i, m: (i, 0))],
        out_specs=pl.BlockSpec(memory_space=pl.ANY),
        scratch_shapes=[pltpu.VMEM((MOE_BLOCK, d), F32), pltpu.SemaphoreType.DMA((2,))],
    )
    return pl.pallas_call(
        functools.partial(_dispatch_kernel, tile=tile, block=MOE_BLOCK,
                          n_blocks=n_rows_padded // MOE_BLOCK),
        grid_spec=grid_spec,
        out_shape=jax.ShapeDtypeStruct((n_rows_padded, d), F32),
        compiler_params=pltpu.CompilerParams(dimension_semantics=("arbitrary",),
                                             vmem_limit_bytes=VMEM_LIMIT),
        name="dispatch",
    )(meta, dest, x1)


def _experts_kernel(be_ref, nb_ref, xs_ref, wg_ref, bg_ref, wu_ref, bu_ref, wd_ref, bd_ref, o_ref,
                    wgb_ref, wub_ref, wdb_ref):
    b = pl.program_id(0)
    prev = be_ref[jnp.maximum(b - 1, 0)]

    @pl.when(jnp.logical_or(b == 0, be_ref[b] != prev))
    def _():
        wgb_ref[...] = wg_ref[0].astype(BF16)
        wub_ref[...] = wu_ref[0].astype(BF16)
        wdb_ref[...] = wd_ref[0].astype(BF16)

    @pl.when(b < nb_ref[0])
    def _():
        xb = xs_ref[...].astype(BF16)
        g = jnp.dot(xb, wgb_ref[...], preferred_element_type=F32) + bg_ref[0]
        u = jnp.dot(xb, wub_ref[...], preferred_element_type=F32) + bu_ref[0]
        g = jnp.minimum(g, SWIGLU_LIMIT)
        u = jnp.clip(u, -SWIGLU_LIMIT, SWIGLU_LIMIT)
        h = (u + 1.0) * (g * _sigmoid(SWIGLU_ALPHA * g))
        o_ref[...] = jnp.dot(h.astype(BF16), wdb_ref[...], preferred_element_type=F32) + bd_ref[0]

    @pl.when(b >= nb_ref[0])
    def _():
        o_ref[...] = jnp.zeros_like(o_ref)


def _experts(block_e, n_used, xs, w_gate, b_gate, w_up, b_up, w_down, b_down):
    n_rows, d = xs.shape
    bm = MOE_BLOCK
    d_e = w_gate.shape[2]
    x_spec = pl.BlockSpec((bm, d), lambda b, be, nb: (b, 0))
    w_spec = lambda a: pl.BlockSpec((1,) + a.shape[1:], lambda b, be, nb: (be[b], 0, 0))
    grid_spec = pltpu.PrefetchScalarGridSpec(
        num_scalar_prefetch=2,
        grid=(n_rows // bm,),
        in_specs=[x_spec, w_spec(w_gate), w_spec(b_gate), w_spec(w_up), w_spec(b_up), w_spec(w_down),
                  w_spec(b_down)],
        out_specs=x_spec,
        scratch_shapes=[pltpu.VMEM((d, d_e), BF16), pltpu.VMEM((d, d_e), BF16), pltpu.VMEM((d_e, d), BF16)],
    )
    return pl.pallas_call(
        _experts_kernel,
        grid_spec=grid_spec,
        out_shape=jax.ShapeDtypeStruct((n_rows, d), F32),
        compiler_params=pltpu.CompilerParams(dimension_semantics=("arbitrary",),
                                             vmem_limit_bytes=VMEM_LIMIT),
        name="experts",
    )(block_e, n_used, xs, w_gate, b_gate, w_up, b_up, w_down, b_down)


def _combine_kernel(dest_ref, x1_ref, route_ref, pp_ref, ps_ref, ys_ref, l2g_ref, l2b_ref, wpg_ref, wpp_ref,
                    y_ref, gbuf_ref, sem_ref, *, tile, n_prompt_tiles, alpha):
    i = pl.program_id(0)

    def row_copy(j, kk):
        return pltpu.make_async_copy(ys_ref.at[pl.ds(dest_ref[j * TOP_K + kk], 1), :],
                                     gbuf_ref.at[kk, pl.ds(j, 1), :], sem_ref.at[0])

    def issue(j, carry):
        for kk in range(TOP_K):
            row_copy(j, kk).start()
        return carry

    def drain(j, carry):
        for kk in range(TOP_K):
            row_copy(j, kk).wait()
        return carry

    lax.fori_loop(0, tile, issue, 0)
    lax.fori_loop(0, tile, drain, 0)

    route = route_ref[...]
    ffn = route[:, 2 * TOP_K:2 * TOP_K + 1] * gbuf_ref[0]
    for kk in range(1, TOP_K):
        ffn = ffn + route[:, 2 * TOP_K + kk:2 * TOP_K + kk + 1] * gbuf_ref[kk]
    x2 = _ln(alpha * x1_ref[...] + ffn, l2g_ref[...], l2b_ref[...])
    p = jnp.where(i >= n_prompt_tiles, ps_ref[...], pp_ref[...]).astype(BF16)
    gate = _sigmoid(jnp.dot(x2.astype(BF16), wpg_ref[...], preferred_element_type=F32))
    y_ref[...] = x2 + gate * jnp.dot(p, wpp_ref[...], preferred_element_type=F32)


def _combine(dest, x1, route, p_p, p_s, ys, l2g, l2b, w_pg, w_pp, n_prompt_rows, alpha):
    rows, d = x1.shape
    tile = MOE_TILE
    n_prompt_tiles = n_prompt_rows // tile
    d_ple = p_p.shape[1]
    row_spec = lambda w: pl.BlockSpec((tile, w), lambda i: (i, 0))
    full = lambda a: pl.BlockSpec(a.shape, lambda i: (0,) * a.ndim)
    return pl.pallas_call(
        functools.partial(_combine_kernel, tile=tile, n_prompt_tiles=n_prompt_tiles, alpha=alpha),
        grid=(rows // tile,),
        in_specs=[pl.BlockSpec((tile * TOP_K,), lambda i: (i,), memory_space=pltpu.SMEM),
                  row_spec(d), row_spec(LANES),
                  pl.BlockSpec((tile, d_ple), lambda i: (jnp.minimum(i, n_prompt_tiles - 1), 0)),
                  pl.BlockSpec((tile, d_ple), lambda i: (jnp.maximum(i - n_prompt_tiles, 0), 0)),
                  pl.BlockSpec(memory_space=pl.ANY), full(l2g), full(l2b), full(w_pg), full(w_pp)],
        out_specs=row_spec(d),
        out_shape=jax.ShapeDtypeStruct((rows, d), F32),
        scratch_shapes=[pltpu.VMEM((TOP_K, tile, d), F32), pltpu.SemaphoreType.DMA((1,))],
        compiler_params=pltpu.CompilerParams(dimension_semantics=("arbitrary",),
                                             vmem_limit_bytes=VMEM_LIMIT),
        name="combine",
    )(dest, x1, route, p_p, p_s, ys, l2g, l2b, w_pg, w_pp)


def kernel(x_prompt, x_sample, cache_k, cache_v, cache_logf, state_conv, page_table, p_prompt, p_sample,
           ln_in_g, ln_in_b, w_in, b_f, conv_w, w_br_conv, w_br_attn, w_o, ln1_g, ln1_b, w_router,
           b_router, w_gate, b_gate, w_up, b_up, w_down, b_down, ln2_g, ln2_b, w_ple_gate, w_ple_proj):
    depth = w_in.shape[0]
    assert depth == 1, "single-layer step only"
    batch, seq, d = x_prompt.shape
    n_seq, t_new, _ = x_sample.shape
    d_conv = conv_w.shape[2]
    n_pool, page = cache_k.shape[1], cache_k.shape[2]
    alpha = (2 * depth) ** 0.25
    n_prompt_rows = batch * seq
    n_sample_rows = n_seq * t_new
    rows = n_prompt_rows + n_sample_rows
    assert seq % ROW_TILE == 0 and n_sample_rows % ROW_TILE == 0 and seq % ATTN_TILE == 0
    assert t_new >= CONV_TAPS - 1

    row2 = lambda a: a.reshape(1, -1)
    x = jnp.concatenate([x_prompt.reshape(n_prompt_rows, d), x_sample.reshape(n_sample_rows, d)], axis=0)

    w = w_in[0]
    o_qkv = 3 * d_conv
    o_f = o_qkv + 3 * D_ATTN
    o_g = o_f + N_HEADS
    w_c = w[:, :o_qkv].astype(BF16)
    w_qkv = w[:, o_qkv:o_f].astype(BF16)
    w_ft = jnp.zeros((2 * SUBLANES, d), BF16).at[:N_HEADS].set(w[:, o_f:o_g].T.astype(BF16))
    w_g = w[:, o_g:].astype(BF16)

    q, k, v, kb, vb, lf_t, c_t = _qkv(x, row2(ln_in_g), row2(ln_in_b), w_qkv, w_ft,
                                      b_f[0].reshape(N_HEADS, 1), seq)

    ya_p = _prompt_attention(q, kb, vb, c_t, batch, seq)

    sl = lambda a: a[n_prompt_rows:].reshape(n_seq, t_new, -1)
    lf_s = jnp.transpose(lf_t[:, n_prompt_rows:].reshape(N_HEADS, n_seq, t_new), (1, 0, 2))
    cache_lft = jnp.transpose(cache_logf[0], (0, 2, 1))
    ya_s = _sample_attention(page_table, sl(q), sl(k), sl(v), lf_s,
                             cache_k[0].reshape(n_pool, page, D_ATTN),
                             cache_v[0].reshape(n_pool, page, D_ATTN), cache_lft)

    st = state_conv[0]
    zeros = jnp.zeros((n_seq, t_new - 2, d_conv), F32)
    tap0 = jnp.concatenate([st, zeros], axis=1).reshape(n_sample_rows, d_conv)
    tap1 = jnp.concatenate([st[:, 1:], jnp.zeros((n_seq, t_new - 1, d_conv), F32)],
                           axis=1).reshape(n_sample_rows, d_conv)

    w_r = jnp.zeros((d, LANES), F32).at[:, :N_EXPERTS].set(w_router[0])
    b_r = jnp.zeros((1, LANES), F32).at[:, :N_EXPERTS].set(b_router[0])
    conv_w8 = jnp.zeros((SUBLANES, d_conv), F32).at[:CONV_TAPS].set(conv_w[0])
    x1, route, counts, u_last, u_s = _mixer(
        x, ya_p, ya_s.reshape(n_sample_rows, D_ATTN), tap0, tap1, row2(ln_in_g), row2(ln_in_b), w_c, w_g,
        conv_w8, w_br_conv[0].astype(BF16), w_br_attn[0].astype(BF16), w_o[0].astype(BF16),
        row2(ln1_g[0]), row2(ln1_b[0]), w_r, b_r, seq, n_prompt_rows, t_new, alpha)

    bm = MOE_BLOCK
    n_assign = rows * TOP_K
    n_blocks = -(-n_assign // bm) + N_EXPERTS
    cnt = counts[0, :N_EXPERTS].astype(I32)
    padded = (cnt + bm - 1) // bm * bm
    pad_end = jnp.cumsum(padded)
    pad_start = pad_end - padded
    idx = route[:, :TOP_K].astype(I32)
    rank = route[:, TOP_K:2 * TOP_K].astype(I32)
    dest = (pad_start[idx] + rank).reshape(-1)
    n_used = (pad_end[-1] // bm).astype(I32).reshape(1)
    blk = jnp.minimum(jnp.arange(n_blocks, dtype=I32), n_used[0] - 1) * bm
    block_e = jnp.minimum(jnp.searchsorted(pad_end, blk, side="right"), N_EXPERTS - 1).astype(I32)
    meta = jnp.concatenate([pad_end, padded, n_used]).astype(I32)

    xs = _dispatch(meta, dest, x1, n_blocks * bm)
    as3 = lambda a: a[0].reshape(N_EXPERTS, 1, -1)
    ys = _experts(block_e, n_used, xs, w_gate[0], as3(b_gate), w_up[0], as3(b_up), w_down[0], as3(b_down))
    y = _combine(dest, x1, route, p_prompt[0].reshape(n_prompt_rows, -1),
                 p_sample[0].reshape(n_sample_rows, -1), ys, row2(ln2_g[0]), row2(ln2_b[0]),
                 w_ple_gate[0].astype(BF16), w_ple_proj[0].astype(BF16), n_prompt_rows, alpha)

    tiles_per_seq = seq // ROW_TILE
    conv_p = u_last[tiles_per_seq - 1::tiles_per_seq][:batch, SUBLANES - 2:]
    conv_s = u_s.reshape(n_seq, t_new, d_conv)[:, t_new - 2:]
    pr = lambda a, *s: a[:n_prompt_rows].reshape((1, batch, seq) + s)
    sr = lambda a, *s: a[n_prompt_rows:].reshape((1, n_seq, t_new) + s)
    lf = lf_t.T
    return (y[:n_prompt_rows].reshape(batch, seq, d), y[n_prompt_rows:].reshape(n_seq, t_new, d),
            pr(k, N_HEADS, HEAD_DIM), pr(v, N_HEADS, HEAD_DIM), pr(lf, N_HEADS), conv_p[None],
            sr(k, N_HEADS, HEAD_DIM), sr(v, N_HEADS, HEAD_DIM), sr(lf, N_HEADS), conv_s[None])
```

```python
import functools

import jax
import jax.numpy as jnp
from jax import lax
from jax.experimental import pallas as pl
from jax.experimental.pallas import tpu as pltpu

F32 = jnp.float32
BF16 = jnp.bfloat16
I32 = jnp.int32

N_HEADS = 8
HEAD_DIM = 64
D_ATTN = N_HEADS * HEAD_DIM
TOP_K = 4
CONV_TAPS = 3
N_EXPERTS = 32
LN_EPS = 1e-5
SWIGLU_LIMIT = 7.0
SWIGLU_ALPHA = 1.702
LANES = 128
SUBLANES = 8
VMEM_LIMIT = 56 * 1024 * 1024

NN = (((1,), (0,)), ((), ()))
NT = (((1,), (1,)), ((), ()))

ROW_TILE = 512
ATTN_Q_TILE = 512
ATTN_K_TILE = 512
MOE_BLOCK = 256
MOE_TILE = 256


def _ln(x, g, b):
    mu = jnp.mean(x, axis=-1, keepdims=True)
    xc = x - mu
    var = jnp.mean(xc * xc, axis=-1, keepdims=True)
    return xc * lax.rsqrt(var + LN_EPS) * g + b


def _log_sigmoid(x):
    return jnp.minimum(x, 0.0) - jnp.log1p(jnp.exp(-jnp.abs(x)))


def _sigmoid(x):
    return 1.0 / (1.0 + jnp.exp(-x))


def _split3(a):
    hi = a.astype(BF16)
    r = a - hi.astype(F32)
    mid = r.astype(BF16)
    lo = (r - mid.astype(F32)).astype(BF16)
    return hi, mid, lo


def _dot_sel(a, sel, dims=NN):
    out = None
    for piece in _split3(a):
        t = lax.dot_general(piece, sel, dims, preferred_element_type=F32)
        out = t if out is None else out + t
    return out


def _sel_dot(sel, a):
    out = None
    for piece in _split3(a):
        t = lax.dot_general(sel, piece, NN, preferred_element_type=F32)
        out = t if out is None else out + t
    return out


def _qkv_kernel(x_ref, g_ref, b_ref, wqkv_ref, wft_ref, bf_ref, triu_ref,
                q_ref, k_ref, v_ref, kb_ref, vb_ref, lf_ref, c_ref, carry_ref, *, tiles_per_seq):
    i = pl.program_id(0)
    xb = _ln(x_ref[...], g_ref[...], b_ref[...]).astype(BF16)
    z = jnp.dot(xb, wqkv_ref[...], preferred_element_type=F32)
    q_ref[...] = z[:, :D_ATTN] * (HEAD_DIM ** -0.5)
    k = z[:, D_ATTN:2 * D_ATTN]
    v = z[:, 2 * D_ATTN:]
    k_ref[...] = k
    v_ref[...] = v
    kb_ref[...] = k.astype(BF16)
    vb_ref[...] = v.astype(BF16)
    ft = lax.dot_general(wft_ref[...], xb, NT, preferred_element_type=F32)[:N_HEADS]
    lf = _log_sigmoid(ft + bf_ref[...])
    lf_ref[...] = lf

    @pl.when(i % tiles_per_seq == 0)
    def _():
        carry_ref[...] = jnp.zeros_like(carry_ref)

    c = _dot_sel(lf, triu_ref[...]) + carry_ref[:, :1]
    c_ref[...] = c
    carry_ref[...] = jnp.broadcast_to(c[:, -1:], carry_ref.shape)


def _qkv(x, ln_g, ln_b, w_qkv, w_ft, b_f, seq):
    rows, d = x.shape
    tm = ROW_TILE
    triu = jnp.triu(jnp.ones((tm, tm), F32)).astype(BF16)
    row_spec = lambda w: pl.BlockSpec((tm, w), lambda i: (i, 0))
    t_spec = pl.BlockSpec((N_HEADS, tm), lambda i: (0, i))
    full = lambda a: pl.BlockSpec(a.shape, lambda i: (0,) * a.ndim)
    return pl.pallas_call(
        functools.partial(_qkv_kernel, tiles_per_seq=seq // tm),
        grid=(rows // tm,),
        in_specs=[row_spec(d), full(ln_g), full(ln_b), full(w_qkv), full(w_ft), full(b_f), full(triu)],
        out_specs=[row_spec(D_ATTN)] * 5 + [t_spec, t_spec],
        out_shape=[jax.ShapeDtypeStruct((rows, D_ATTN), F32)] * 3
        + [jax.ShapeDtypeStruct((rows, D_ATTN), BF16)] * 2
        + [jax.ShapeDtypeStruct((N_HEADS, rows), F32)] * 2,
        scratch_shapes=[pltpu.VMEM((N_HEADS, LANES), F32)],
        compiler_params=pltpu.CompilerParams(dimension_semantics=("arbitrary",),
                                             vmem_limit_bytes=VMEM_LIMIT),
        name="qkv",
    )(x, ln_g, ln_b, w_qkv, w_ft, b_f, triu)


def _pattn_kernel(q_ref, k_ref, v_ref, c_ref, o_ref, *, tq, tk):
    hp = pl.program_id(1)
    qi = pl.program_id(2)
    q = q_ref[...].astype(BF16)
    lane = lax.broadcasted_iota(I32, (tq, LANES), 1)
    row = lax.broadcasted_iota(I32, (tq, tk), 0)
    col = lax.broadcasted_iota(I32, (tq, tk), 1)
    zero = jnp.zeros_like(q)
    q_heads = (jnp.where(lane < HEAD_DIM, q, zero), jnp.where(lane >= HEAD_DIM, q, zero))
    sub = tq // tk

    def step(kj, carry, diag):
        ks = pl.multiple_of(kj * tk, tk)
        kt = k_ref[pl.ds(ks, tk), :]
        vt = v_ref[pl.ds(ks, tk), :]
        new = []
        for hh in range(2):
            m, l, acc = carry[hh]
            s = lax.dot_general(q_heads[hh], kt, NT, preferred_element_type=F32)
            s = s - c_ref[pl.ds(2 * hp + hh, 1), pl.ds(ks, tk)]
            if diag is not None:
                s = jnp.where(col + diag * tk <= row, s, -jnp.inf)
            m_new = jnp.maximum(m, jnp.max(s, axis=-1, keepdims=True))
            alpha = jnp.exp(m - m_new)
            p = jnp.exp(s - m_new)
            l = alpha * l + jnp.sum(p, axis=-1, keepdims=True)
            acc = alpha * acc + jnp.dot(p.astype(BF16), vt, preferred_element_type=F32)
            new.append((m_new, l, acc))
        return tuple(new)

    init = (jnp.full((tq, 1), -jnp.inf, F32), jnp.zeros((tq, 1), F32),
            jnp.zeros((tq, LANES), F32))
    carry = lax.fori_loop(0, qi * sub, functools.partial(step, diag=None), (init, init))
    for d in range(sub):
        carry = step(qi * sub + d, carry, d)
    (_, l0, acc0), (_, l1, acc1) = carry
    o_ref[...] = jnp.where(lane < HEAD_DIM, acc0 / l0, acc1 / l1).astype(BF16)


def _prompt_attention(q, kb, vb, c_t, batch, seq):
    tq, tk = ATTN_Q_TILE, ATTN_K_TILE
    nq = seq // tq
    return pl.pallas_call(
        functools.partial(_pattn_kernel, tq=tq, tk=tk),
        grid=(batch, D_ATTN // LANES, nq),
        in_specs=[
            pl.BlockSpec((tq, LANES), lambda b, h, i: (b * nq + i, h)),
            pl.BlockSpec((seq, LANES), lambda b, h, i: (b, h)),
            pl.BlockSpec((seq, LANES), lambda b, h, i: (b, h)),
            pl.BlockSpec((N_HEADS, seq), lambda b, h, i: (0, b)),
        ],
        out_specs=pl.BlockSpec((tq, LANES), lambda b, h, i: (b * nq + i, h)),
        out_shape=jax.ShapeDtypeStruct((batch * seq, D_ATTN), BF16),
        compiler_params=pltpu.CompilerParams(
            dimension_semantics=("arbitrary", "arbitrary", "arbitrary"),
            vmem_limit_bytes=VMEM_LIMIT),
        name="pattn",
    )(q, kb, vb, c_t)


def _sattn_kernel(pt_ref, q_ref, kn_ref, vn_ref, lfn_ref, triu_ref, pagesel_ref, *rest,
                  n_pages, page, t_new):
    del pt_ref
    k_refs = rest[:n_pages]
    v_refs = rest[n_pages:2 * n_pages]
    f_refs = rest[2 * n_pages:3 * n_pages]
    o_ref = rest[3 * n_pages]
    nrow = t_new * N_HEADS
    q = q_ref[0]
    qrep = jnp.concatenate([jnp.broadcast_to(q[t:t + 1], (N_HEADS, D_ATTN)) for t in range(t_new)],
                           axis=0)
    rowi = lax.broadcasted_iota(I32, (nrow, D_ATTN), 0)
    lanei = lax.broadcasted_iota(I32, (nrow, D_ATTN), 1)
    head_mask = (lanei // HEAD_DIM) == (rowi % N_HEADS)
    qexp = jnp.where(head_mask, qrep, 0.0).astype(BF16)
    qexp_f = qexp.astype(F32)

    lf = jnp.concatenate([f_refs[p][0] for p in range(n_pages)], axis=0)
    c_in = _dot_sel(lf, triu_ref[...])
    tot = jnp.broadcast_to(c_in[:, page - 1:page], (n_pages * N_HEADS, page))
    c_all = c_in + _sel_dot(pagesel_ref[...], tot)
    c_tot = c_all[(n_pages - 1) * N_HEADS:, page - 1:page]

    logits = []
    for p in range(n_pages):
        kp = k_refs[p][0].astype(BF16)
        s = lax.dot_general(qexp, kp, NT, preferred_element_type=F32)
        cp = c_all[p * N_HEADS:(p + 1) * N_HEADS]
        logits.append(s - jnp.concatenate([cp] * t_new, axis=0))
    m = logits[0]
    for s in logits[1:]:
        m = jnp.maximum(m, s)
    m = jnp.max(m, axis=-1, keepdims=True)

    kn = kn_ref[0].astype(BF16).astype(F32)
    vn = vn_ref[0].astype(BF16).astype(F32)
    lfn = lfn_ref[0]
    trow = lax.broadcasted_iota(I32, (nrow, 1), 0) // N_HEADS
    new_logits = []
    cn = c_tot
    for j in range(t_new):
        cn = cn + lfn[:, j:j + 1]
        sj = jnp.sum(qexp_f * kn[j:j + 1], axis=-1, keepdims=True)
        sj = sj - jnp.concatenate([cn] * t_new, axis=0)
        sj = jnp.where(trow >= j, sj, -jnp.inf)
        new_logits.append(sj)
        m = jnp.maximum(m, sj)

    l = jnp.zeros((nrow, 1), F32)
    acc = jnp.zeros((nrow, D_ATTN), F32)
    for p in range(n_pages):
        pr = jnp.exp(logits[p] - m)
        l = l + jnp.sum(pr, axis=-1, keepdims=True)
        acc = acc + jnp.dot(pr.astype(BF16), v_refs[p][0].astype(BF16), preferred_element_type=F32)
    for j in range(t_new):
        pj = jnp.exp(new_logits[j] - m)
        l = l + pj
        acc = acc + pj.astype(BF16).astype(F32) * vn[j:j + 1]
    acc = jnp.where(head_mask, acc / l, 0.0)
    o_ref[0] = jnp.concatenate(
        [jnp.sum(acc[t * N_HEADS:(t + 1) * N_HEADS], axis=0, keepdims=True) for t in range(t_new)], axis=0)


def _sample_attention(page_table, q_s, k_s, v_s, lf_s, cache_k, cache_v, cache_lft):
    n_seq, n_pages = page_table.shape
    page = cache_k.shape[1]
    t_new = q_s.shape[1]
    triu = jnp.triu(jnp.ones((page, page), F32)).astype(BF16)
    r = jnp.arange(n_pages * N_HEADS)
    pagesel = ((r[:, None] % N_HEADS == r[None, :] % N_HEADS)
               & (r[None, :] // N_HEADS < r[:, None] // N_HEADS)).astype(BF16)

    def page_spec(p, shape):
        return pl.BlockSpec((1,) + shape, lambda n, pt, p=p: (pt[n * n_pages + p], 0, 0))

    seq_spec = lambda a: pl.BlockSpec((1,) + a.shape[1:], lambda n, pt: (n, 0, 0))
    const = lambda a: pl.BlockSpec(a.shape, lambda n, pt: (0, 0))
    grid_spec = pltpu.PrefetchScalarGridSpec(
        num_scalar_prefetch=1,
        grid=(n_seq,),
        in_specs=[seq_spec(q_s), seq_spec(k_s), seq_spec(v_s), seq_spec(lf_s), const(triu), const(pagesel)]
        + [page_spec(p, (page, D_ATTN)) for p in range(n_pages)]
        + [page_spec(p, (page, D_ATTN)) for p in range(n_pages)]
        + [page_spec(p, (N_HEADS, page)) for p in range(n_pages)],
        out_specs=pl.BlockSpec((1, t_new, D_ATTN), lambda n, pt: (n, 0, 0)),
    )
    return pl.pallas_call(
        functools.partial(_sattn_kernel, n_pages=n_pages, page=page, t_new=t_new),
        grid_spec=grid_spec,
        out_shape=jax.ShapeDtypeStruct((n_seq, t_new, D_ATTN), F32),
        compiler_params=pltpu.CompilerParams(dimension_semantics=("arbitrary",),
                                             vmem_limit_bytes=VMEM_LIMIT),
        name="sattn",
    )(page_table.reshape(-1), q_s, k_s, v_s, lf_s, triu, pagesel,
      *([cache_k] * n_pages), *([cache_v] * n_pages), *([cache_lft] * n_pages))


def _mixer_kernel(x_ref, yap_ref, yas_ref, tap0_ref, tap1_ref, lng_ref, lnb_ref, wc_ref, wg_ref,
                  cw_ref, wbc_ref, wba_ref, wo_ref, l1g_ref, l1b_ref, wr_ref, br_ref, tril_ref,
                  x1_ref, route_ref, cnt_ref, ulast_ref, us_ref,
                  ext_ref, cnt_acc_ref, *, tiles_per_seq, n_prompt_tiles, t_new, alpha):
    i = pl.program_id(0)
    tm = x_ref.shape[0]
    d_conv = us_ref.shape[1]
    is_sample = i >= n_prompt_tiles
    xn = _ln(x_ref[...], lng_ref[...], lnb_ref[...])
    xb = xn.astype(BF16)

    zc = jnp.dot(xb, wc_ref[...], preferred_element_type=F32)
    u = zc[:, d_conv:2 * d_conv] * zc[:, 2 * d_conv:]

    @pl.when(i % tiles_per_seq == 0)
    def _():
        ext_ref[0:SUBLANES, :] = jnp.zeros((SUBLANES, d_conv), F32)

    ext_ref[SUBLANES:, :] = u
    prev1 = ext_ref[SUBLANES - 1:SUBLANES - 1 + tm, :]
    prev2 = ext_ref[SUBLANES - 2:SUBLANES - 2 + tm, :]
    ext_ref[0:SUBLANES, :] = u[tm - SUBLANES:]
    t_in_seq = lax.broadcasted_iota(I32, (tm, 1), 0) % t_new
    keep1 = jnp.logical_or(jnp.logical_not(is_sample), t_in_seq >= 1)
    keep2 = jnp.logical_or(jnp.logical_not(is_sample), t_in_seq >= 2)
    prev1 = jnp.where(keep1, prev1, tap1_ref[...])
    prev2 = jnp.where(keep2, prev2, tap0_ref[...])
    cw = cw_ref[...]
    y_conv = zc[:, :d_conv] * (prev2 * cw[0:1] + prev1 * cw[1:2] + u * cw[2:3])
    ulast_ref[0] = u[tm - SUBLANES:]
    us_ref[...] = u

    ya = jnp.where(is_sample, yas_ref[...].astype(BF16), yap_ref[...])
    bc = jnp.dot(y_conv.astype(BF16), wbc_ref[...], preferred_element_type=F32)
    ba = jnp.dot(ya, wba_ref[...], preferred_element_type=F32)
    g = jnp.dot(xb, wg_ref[...], preferred_element_type=F32)
    d_model = bc.shape[1]
    merged = _sigmoid(g[:, :d_model]) * bc + _sigmoid(g[:, d_model:]) * ba
    mix = jnp.dot(merged.astype(BF16), wo_ref[...], preferred_element_type=F32)
    x1 = _ln(alpha * xn + mix, l1g_ref[...], l1b_ref[...])
    x1_ref[...] = x1

    logits = jnp.dot(x1, wr_ref[...], preferred_element_type=F32,
                     precision=lax.Precision.HIGHEST) + br_ref[...]
    lane = lax.broadcasted_iota(I32, (tm, LANES), 1)
    lane_f = lane.astype(F32)
    logits = jnp.where(lane < N_EXPERTS, logits, -jnp.inf)
    vals, hots = [], []
    for _ in range(TOP_K):
        mx = jnp.max(logits, axis=-1, keepdims=True)
        idx = jnp.min(jnp.where(logits == mx, lane_f, float(LANES)), axis=-1, keepdims=True)
        hot = lane_f == idx
        logits = jnp.where(hot, -jnp.inf, logits)
        vals.append(mx)
        hots.append(hot)
    exps = [jnp.exp(v - vals[0]) for v in vals]
    denom = exps[0]
    for e in exps[1:]:
        denom = denom + e

    @pl.when(i == 0)
    def _():
        cnt_acc_ref[...] = jnp.zeros_like(cnt_acc_ref)

    chosen = hots[0]
    for h in hots[1:]:
        chosen = jnp.logical_or(chosen, h)
    chosen_f = jnp.where(chosen, 1.0, 0.0)
    before = jnp.dot(tril_ref[...], chosen_f.astype(BF16), preferred_element_type=F32) + cnt_acc_ref[0:1, :]
    route = jnp.zeros((tm, LANES), F32)
    for kk in range(TOP_K):
        idx_f = jnp.sum(jnp.where(hots[kk], lane_f, 0.0), axis=-1, keepdims=True)
        rank = jnp.sum(jnp.where(hots[kk], before, 0.0), axis=-1, keepdims=True)
        route = jnp.where(lane == kk, idx_f, route)
        route = jnp.where(lane == TOP_K + kk, rank, route)
        route = jnp.where(lane == 2 * TOP_K + kk, exps[kk] / denom, route)
    route_ref[...] = route
    cnt = cnt_acc_ref[0:1, :] + jnp.sum(chosen_f, axis=0, keepdims=True)
    cnt_acc_ref[...] = jnp.broadcast_to(cnt, cnt_acc_ref.shape)
    cnt_ref[...] = jnp.broadcast_to(cnt, cnt_ref.shape)


def _mixer(x, ya_p, ya_s, tap0, tap1, ln_g, ln_b, w_c, w_g, conv_w, w_bc, w_ba, w_o, l1g, l1b,
           w_r, b_r, seq, n_prompt_rows, t_new, alpha):
    rows, d = x.shape
    tm = ROW_TILE
    n_prompt_tiles = n_prompt_rows // tm
    n_tiles = rows // tm
    d_conv = w_bc.shape[0]
    tril = jnp.tril(jnp.ones((tm, tm), F32), -1).astype(BF16)
    row_spec = lambda w: pl.BlockSpec((tm, w), lambda i: (i, 0))
    prompt_spec = lambda w: pl.BlockSpec((tm, w), lambda i: (jnp.minimum(i, n_prompt_tiles - 1), 0))
    sample_spec = lambda w: pl.BlockSpec((tm, w), lambda i: (jnp.maximum(i - n_prompt_tiles, 0), 0))
    full = lambda a: pl.BlockSpec(a.shape, lambda i: (0,) * a.ndim)
    weights = [ln_g, ln_b, w_c, w_g, conv_w, w_bc, w_ba, w_o, l1g, l1b, w_r, b_r, tril]
    n_sample_rows = rows - n_prompt_rows
    return pl.pallas_call(
        functools.partial(_mixer_kernel, tiles_per_seq=seq // tm, n_prompt_tiles=n_prompt_tiles,
                          t_new=t_new, alpha=alpha),
        grid=(n_tiles,),
        in_specs=[row_spec(d), prompt_spec(D_ATTN), sample_spec(D_ATTN), sample_spec(d_conv),
                  sample_spec(d_conv)] + [full(w) for w in weights],
        out_specs=[row_spec(d), row_spec(LANES), pl.BlockSpec((SUBLANES, LANES), lambda i: (0, 0)),
                   pl.BlockSpec((1, SUBLANES, d_conv), lambda i: (i, 0, 0)), sample_spec(d_conv)],
        out_shape=[jax.ShapeDtypeStruct((rows, d), F32), jax.ShapeDtypeStruct((rows, LANES), F32),
                   jax.ShapeDtypeStruct((SUBLANES, LANES), F32),
                   jax.ShapeDtypeStruct((n_tiles, SUBLANES, d_conv), F32),
                   jax.ShapeDtypeStruct((n_sample_rows, d_conv), F32)],
        scratch_shapes=[pltpu.VMEM((tm + SUBLANES, d_conv), F32), pltpu.VMEM((SUBLANES, LANES), F32)],
        compiler_params=pltpu.CompilerParams(dimension_semantics=("arbitrary",),
                                             vmem_limit_bytes=VMEM_LIMIT),
        name="mixer",
    )(x, ya_p, ya_s, tap0, tap1, *weights)


def _dispatch_kernel(meta_ref, dest_ref, x_ref, xs_ref, zero_ref, sem_ref, *, tile, block, n_blocks):
    i = pl.program_id(0)

    def row_copy(j, kk):
        return pltpu.make_async_copy(x_ref.at[pl.ds(j, 1), :],
                                     xs_ref.at[pl.ds(dest_ref[j * TOP_K + kk], 1), :], sem_ref.at[0])

    def zero_copy(start):
        return pltpu.make_async_copy(zero_ref, xs_ref.at[pl.ds(pl.multiple_of(start, block), block), :],
                                     sem_ref.at[1])

    def pad_copy(e):
        return zero_copy(meta_ref[e] - block)

    @pl.when(i == 0)
    def _():
        zero_ref[...] = jnp.zeros_like(zero_ref)
        n_used = meta_ref[2 * N_EXPERTS]
        for e in range(N_EXPERTS):
            @pl.when(meta_ref[N_EXPERTS + e] > 0)
            def _():
                pad_copy(e).start()

        def tail_start(b, carry):
            zero_copy(b * block).start()
            return carry

        def tail_wait(b, carry):
            zero_copy(b * block).wait()
            return carry

        lax.fori_loop(n_used, n_blocks, tail_start, 0)
        for e in range(N_EXPERTS):
            @pl.when(meta_ref[N_EXPERTS + e] > 0)
            def _():
                pad_copy(e).wait()
        lax.fori_loop(n_used, n_blocks, tail_wait, 0)

    def issue(j, carry):
        for kk in range(TOP_K):
            row_copy(j, kk).start()
        return carry

    def drain(j, carry):
        for kk in range(TOP_K):
            row_copy(j, kk).wait()
        return carry

    lax.fori_loop(0, tile, issue, 0)
    lax.fori_loop(0, tile, drain, 0)


def _dispatch(meta, dest, x1, n_rows_padded):
    rows, d = x1.shape
    tile = MOE_TILE
    grid_spec = pltpu.PrefetchScalarGridSpec(
        num_scalar_prefetch=1,
        grid=(rows // tile,),
        in_specs=[pl.BlockSpec((tile * TOP_K,), lambda i, m: (i,), memory_space=pltpu.SMEM),
                  pl.BlockSpec((tile, d), lambda i, m: (i, 0))],
        out_specs=pl.BlockSpec(memory_space=pl.ANY),
        scratch_shapes=[pltpu.VMEM((MOE_BLOCK, d), F32), pltpu.SemaphoreType.DMA((2,))],
    )
    return pl.pallas_call(
        functools.partial(_dispatch_kernel, tile=tile, block=MOE_BLOCK,
                          n_blocks=n_rows_padded // MOE_BLOCK),
        grid_spec=grid_spec,
        out_shape=jax.ShapeDtypeStruct((n_rows_padded, d), F32),
        compiler_params=pltpu.CompilerParams(dimension_semantics=("arbitrary",),
                                             vmem_limit_bytes=VMEM_LIMIT),
        name="dispatch",
    )(meta, dest, x1)


def _experts_kernel(be_ref, nb_ref, xs_ref, wg_ref, bg_ref, wu_ref, bu_ref, wd_ref, bd_ref, o_ref,
                    wgb_ref, wub_ref, wdb_ref):
    b = pl.program_id(0)
    prev = be_ref[jnp.maximum(b - 1, 0)]

    @pl.when(jnp.logical_or(b == 0, be_ref[b] != prev))
    def _():
        wgb_ref[...] = wg_ref[0].astype(BF16)
        wub_ref[...] = wu_ref[0].astype(BF16)
        wdb_ref[...] = wd_ref[0].astype(BF16)

    @pl.when(b < nb_ref[0])
    def _():
        xb = xs_ref[...].astype(BF16)
        g = jnp.dot(xb, wgb_ref[...], preferred_element_type=F32) + bg_ref[0]
        u = jnp.dot(xb, wub_ref[...], preferred_element_type=F32) + bu_ref[0]
        g = jnp.minimum(g, SWIGLU_LIMIT)
        u = jnp.clip(u, -SWIGLU_LIMIT, SWIGLU_LIMIT)
        h = (u + 1.0) * (g * _sigmoid(SWIGLU_ALPHA * g))
        o_ref[...] = jnp.dot(h.astype(BF16), wdb_ref[...], preferred_element_type=F32) + bd_ref[0]

    @pl.when(b >= nb_ref[0])
    def _():
        o_ref[...] = jnp.zeros_like(o_ref)


def _experts(block_e, n_used, xs, w_gate, b_gate, w_up, b_up, w_down, b_down):
    n_rows, d = xs.shape
    bm = MOE_BLOCK
    d_e = w_gate.shape[2]
    x_spec = pl.BlockSpec((bm, d), lambda b, be, nb: (b, 0))
    w_spec = lambda a: pl.BlockSpec((1,) + a.shape[1:], lambda b, be, nb: (be[b], 0, 0))
    grid_spec = pltpu.PrefetchScalarGridSpec(
        num_scalar_prefetch=2,
        grid=(n_rows // bm,),
        in_specs=[x_spec, w_spec(w_gate), w_spec(b_gate), w_spec(w_up), w_spec(b_up), w_spec(w_down),
                  w_spec(b_down)],
        out_specs=x_spec,
        scratch_shapes=[pltpu.VMEM((d, d_e), BF16), pltpu.VMEM((d, d_e), BF16), pltpu.VMEM((d_e, d), BF16)],
    )
    return pl.pallas_call(
        _experts_kernel,
        grid_spec=grid_spec,
        out_shape=jax.ShapeDtypeStruct((n_rows, d), F32),
        compiler_params=pltpu.CompilerParams(dimension_semantics=("arbitrary",),
                                             vmem_limit_bytes=VMEM_LIMIT),
        name="experts",
    )(block_e, n_used, xs, w_gate, b_gate, w_up, b_up, w_down, b_down)


def _combine_kernel(dest_ref, x1_ref, route_ref, pp_ref, ps_ref, ys_ref, l2g_ref, l2b_ref, wpg_ref, wpp_ref,
                    y_ref, gbuf_ref, sem_ref, *, tile, n_prompt_tiles, alpha):
    i = pl.program_id(0)

    def row_copy(j, kk):
        return pltpu.make_async_copy(ys_ref.at[pl.ds(dest_ref[j * TOP_K + kk], 1), :],
                                     gbuf_ref.at[kk, pl.ds(j, 1), :], sem_ref.at[0])

    def issue(j, carry):
        for kk in range(TOP_K):
            row_copy(j, kk).start()
        return carry

    def drain(j, carry):
        for kk in range(TOP_K):
            row_copy(j, kk).wait()
        return carry

    lax.fori_loop(0, tile, issue, 0)
    lax.fori_loop(0, tile, drain, 0)

    route = route_ref[...]
    ffn = route[:, 2 * TOP_K:2 * TOP_K + 1] * gbuf_ref[0]
    for kk in range(1, TOP_K):
        ffn = ffn + route[:, 2 * TOP_K + kk:2 * TOP_K + kk + 1] * gbuf_ref[kk]
    x2 = _ln(alpha * x1_ref[...] + ffn, l2g_ref[...], l2b_ref[...])
    p = jnp.where(i >= n_prompt_tiles, ps_ref[...], pp_ref[...]).astype(BF16)
    gate = _sigmoid(jnp.dot(x2.astype(BF16), wpg_ref[...], preferred_element_type=F32))
    y_ref[...] = x2 + gate * jnp.dot(p, wpp_ref[...], preferred_element_type=F32)


def _combine(dest, x1, route, p_p, p_s, ys, l2g, l2b, w_pg, w_pp, n_prompt_rows, alpha):
    rows, d = x1.shape
    tile = MOE_TILE
    n_prompt_tiles = n_prompt_rows // tile
    d_ple = p_p.shape[1]
    row_spec = lambda w: pl.BlockSpec((tile, w), lambda i: (i, 0))
    full = lambda a: pl.BlockSpec(a.shape, lambda i: (0,) * a.ndim)
    return pl.pallas_call(
        functools.partial(_combine_kernel, tile=tile, n_prompt_tiles=n_prompt_tiles, alpha=alpha),
        grid=(rows // tile,),
        in_specs=[pl.BlockSpec((tile * TOP_K,), lambda i: (i,), memory_space=pltpu.SMEM),
                  row_spec(d), row_spec(LANES),
                  pl.BlockSpec((tile, d_ple), lambda i: (jnp.minimum(i, n_prompt_tiles - 1), 0)),
                  pl.BlockSpec((tile, d_ple), lambda i: (jnp.maximum(i - n_prompt_tiles, 0), 0)),
                  pl.BlockSpec(memory_space=pl.ANY), full(l2g), full(l2b), full(w_pg), full(w_pp)],
        out_specs=row_spec(d),
        out_shape=jax.ShapeDtypeStruct((rows, d), F32),
        scratch_shapes=[pltpu.VMEM((TOP_K, tile, d), F32), pltpu.SemaphoreType.DMA((1,))],
        compiler_params=pltpu.CompilerParams(dimension_semantics=("arbitrary",),
                                             vmem_limit_bytes=VMEM_LIMIT),
        name="combine",
    )(dest, x1, route, p_p, p_s, ys, l2g, l2b, w_pg, w_pp)


def kernel(x_prompt, x_sample, cache_k, cache_v, cache_logf, state_conv, page_table, p_prompt, p_sample,
           ln_in_g, ln_in_b, w_in, b_f, conv_w, w_br_conv, w_br_attn, w_o, ln1_g, ln1_b, w_router,
           b_router, w_gate, b_gate, w_up, b_up, w_down, b_down, ln2_g, ln2_b, w_ple_gate, w_ple_proj):
    depth = w_in.shape[0]
    assert depth == 1, "single-layer step only"
    batch, seq, d = x_prompt.shape
    n_seq, t_new, _ = x_sample.shape
    d_conv = conv_w.shape[2]
    n_pool, page = cache_k.shape[1], cache_k.shape[2]
    alpha = (2 * depth) ** 0.25
    n_prompt_rows = batch * seq
    n_sample_rows = n_seq * t_new
    rows = n_prompt_rows + n_sample_rows
    assert seq % ROW_TILE == 0 and n_sample_rows % ROW_TILE == 0 and seq % ATTN_Q_TILE == 0
    assert ATTN_Q_TILE % ATTN_K_TILE == 0
    assert t_new >= CONV_TAPS - 1

    row2 = lambda a: a.reshape(1, -1)
    x = jnp.concatenate([x_prompt.reshape(n_prompt_rows, d), x_sample.reshape(n_sample_rows, d)], axis=0)

    w = w_in[0]
    o_qkv = 3 * d_conv
    o_f = o_qkv + 3 * D_ATTN
    o_g = o_f + N_HEADS
    w_c = w[:, :o_qkv].astype(BF16)
    w_qkv = w[:, o_qkv:o_f].astype(BF16)
    w_ft = jnp.zeros((2 * SUBLANES, d), BF16).at[:N_HEADS].set(w[:, o_f:o_g].T.astype(BF16))
    w_g = w[:, o_g:].astype(BF16)

    q, k, v, kb, vb, lf_t, c_t = _qkv(x, row2(ln_in_g), row2(ln_in_b), w_qkv, w_ft,
                                      b_f[0].reshape(N_HEADS, 1), seq)

    ya_p = _prompt_attention(q, kb, vb, c_t, batch, seq)

    sl = lambda a: a[n_prompt_rows:].reshape(n_seq, t_new, -1)
    lf_s = jnp.transpose(lf_t[:, n_prompt_rows:].reshape(N_HEADS, n_seq, t_new), (1, 0, 2))
    cache_lft = jnp.transpose(cache_logf[0], (0, 2, 1))
    ya_s = _sample_attention(page_table, sl(q), sl(k), sl(v), lf_s,
                             cache_k[0].reshape(n_pool, page, D_ATTN),
                             cache_v[0].reshape(n_pool, page, D_ATTN), cache_lft)

    st = state_conv[0]
    zeros = jnp.zeros((n_seq, t_new - 2, d_conv), F32)
    tap0 = jnp.concatenate([st, zeros], axis=1).reshape(n_sample_rows, d_conv)
    tap1 = jnp.concatenate([st[:, 1:], jnp.zeros((n_seq, t_new - 1, d_conv), F32)],
                           axis=1).reshape(n_sample_rows, d_conv)

    w_r = jnp.zeros((d, LANES), F32).at[:, :N_EXPERTS].set(w_router[0])
    b_r = jnp.zeros((1, LANES), F32).at[:, :N_EXPERTS].set(b_router[0])
    conv_w8 = jnp.zeros((SUBLANES, d_conv), F32).at[:CONV_TAPS].set(conv_w[0])
    x1, route, counts, u_last, u_s = _mixer(
        x, ya_p, ya_s.reshape(n_sample_rows, D_ATTN), tap0, tap1, row2(ln_in_g), row2(ln_in_b), w_c, w_g,
        conv_w8, w_br_conv[0].astype(BF16), w_br_attn[0].astype(BF16), w_o[0].astype(BF16),
        row2(ln1_g[0]), row2(ln1_b[0]), w_r, b_r, seq, n_prompt_rows, t_new, alpha)

    bm = MOE_BLOCK
    n_assign = rows * TOP_K
    n_blocks = -(-n_assign // bm) + N_EXPERTS
    cnt = counts[0, :N_EXPERTS].astype(I32)
    padded = (cnt + bm - 1) // bm * bm
    pad_end = jnp.cumsum(padded)
    pad_start = pad_end - padded
    idx = route[:, :TOP_K].astype(I32)
    rank = route[:, TOP_K:2 * TOP_K].astype(I32)
    dest = (pad_start[idx] + rank).reshape(-1)
    n_used = (pad_end[-1] // bm).astype(I32).reshape(1)
    blk = jnp.minimum(jnp.arange(n_blocks, dtype=I32), n_used[0] - 1) * bm
    block_e = jnp.minimum(jnp.sum(blk[:, None] >= pad_end[None, :], axis=1), N_EXPERTS - 1).astype(I32)
    meta = jnp.concatenate([pad_end, padded, n_used]).astype(I32)

    xs = _dispatch(meta, dest, x1, n_blocks * bm)
    as3 = lambda a: a[0].reshape(N_EXPERTS, 1, -1)
    ys = _experts(block_e, n_used, xs, w_gate[0], as3(b_gate), w_up[0], as3(b_up), w_down[0], as3(b_down))
    y = _combine(dest, x1, route, p_prompt[0].reshape(n_prompt_rows, -1),
                 p_sample[0].reshape(n_sample_rows, -1), ys, row2(ln2_g[0]), row2(ln2_b[0]),
                 w_ple_gate[0].astype(BF16), w_ple_proj[0].astype(BF16), n_prompt_rows, alpha)

    tiles_per_seq = seq // ROW_TILE
    conv_p = u_last[tiles_per_seq - 1::tiles_per_seq][:batch, SUBLANES - 2:]
    conv_s = u_s.reshape(n_seq, t_new, d_conv)[:, t_new - 2:]
    pr = lambda a, *s: a[:n_prompt_rows].reshape((1, batch, seq) + s)
    sr = lambda a, *s: a[n_prompt_rows:].reshape((1, n_seq, t_new) + s)
    lf = lf_t.T
    return (y[:n_prompt_rows].reshape(batch, seq, d), y[n_prompt_rows:].reshape(n_seq, t_new, d),
            pr(k, N_HEADS, HEAD_DIM), pr(v, N_HEADS, HEAD_DIM), pr(lf, N_HEADS), conv_p[None],
            sr(k, N_HEADS, HEAD_DIM), sr(v, N_HEADS, HEAD_DIM), sr(lf, N_HEADS), conv_s[None])
```

```python
import functools

import jax
import jax.numpy as jnp
from jax import lax
from jax.experimental import pallas as pl
from jax.experimental.pallas import tpu as pltpu

F32 = jnp.float32
BF16 = jnp.bfloat16
I32 = jnp.int32

N_HEADS = 8
HEAD_DIM = 64
D_ATTN = N_HEADS * HEAD_DIM
TOP_K = 4
CONV_TAPS = 3
N_EXPERTS = 32
LN_EPS = 1e-5
SWIGLU_LIMIT = 7.0
SWIGLU_ALPHA = 1.702
LANES = 128
SUBLANES = 8
VMEM_LIMIT = 56 * 1024 * 1024

NN = (((1,), (0,)), ((), ()))
NT = (((1,), (1,)), ((), ()))

ROW_TILE = 512
ATTN_Q_TILE = 512
ATTN_K_TILE = 512
MOE_BLOCK = 256
MOE_TILE = 256


def _ln(x, g, b):
    mu = jnp.mean(x, axis=-1, keepdims=True)
    xc = x - mu
    var = jnp.mean(xc * xc, axis=-1, keepdims=True)
    return xc * lax.rsqrt(var + LN_EPS) * g + b


def _log_sigmoid(x):
    return jnp.minimum(x, 0.0) - jnp.log1p(jnp.exp(-jnp.abs(x)))


def _sigmoid(x):
    return 1.0 / (1.0 + jnp.exp(-x))


def _split3(a):
    hi = a.astype(BF16)
    r = a - hi.astype(F32)
    mid = r.astype(BF16)
    lo = (r - mid.astype(F32)).astype(BF16)
    return hi, mid, lo


def _dot_sel(a, sel, dims=NN):
    out = None
    for piece in _split3(a):
        t = lax.dot_general(piece, sel, dims, preferred_element_type=F32)
        out = t if out is None else out + t
    return out


def _sel_dot(sel, a):
    out = None
    for piece in _split3(a):
        t = lax.dot_general(sel, piece, NN, preferred_element_type=F32)
        out = t if out is None else out + t
    return out


def _qkv_kernel(x_ref, g_ref, b_ref, wqkv_ref, wft_ref, bf_ref, triu_ref,
                q_ref, k_ref, v_ref, kb_ref, vb_ref, lf_ref, c_ref, carry_ref, *, tiles_per_seq):
    i = pl.program_id(0)
    xb = _ln(x_ref[...], g_ref[...], b_ref[...]).astype(BF16)
    z = jnp.dot(xb, wqkv_ref[...], preferred_element_type=F32)
    q_ref[...] = z[:, :D_ATTN] * (HEAD_DIM ** -0.5)
    k = z[:, D_ATTN:2 * D_ATTN]
    v = z[:, 2 * D_ATTN:]
    k_ref[...] = k
    v_ref[...] = v
    kb_ref[...] = k.astype(BF16)
    vb_ref[...] = v.astype(BF16)
    ft = lax.dot_general(wft_ref[...], xb, NT, preferred_element_type=F32)[:N_HEADS]
    lf = _log_sigmoid(ft + bf_ref[...])
    lf_ref[...] = lf

    @pl.when(i % tiles_per_seq == 0)
    def _():
        carry_ref[...] = jnp.zeros_like(carry_ref)

    c = _dot_sel(lf, triu_ref[...]) + carry_ref[:, :1]
    c_ref[...] = c
    carry_ref[...] = jnp.broadcast_to(c[:, -1:], carry_ref.shape)


def _qkv(x, ln_g, ln_b, w_qkv, w_ft, b_f, seq):
    rows, d = x.shape
    tm = ROW_TILE
    triu = jnp.triu(jnp.ones((tm, tm), F32)).astype(BF16)
    row_spec = lambda w: pl.BlockSpec((tm, w), lambda i: (i, 0))
    t_spec = pl.BlockSpec((N_HEADS, tm), lambda i: (0, i))
    full = lambda a: pl.BlockSpec(a.shape, lambda i: (0,) * a.ndim)
    return pl.pallas_call(
        functools.partial(_qkv_kernel, tiles_per_seq=seq // tm),
        grid=(rows // tm,),
        in_specs=[row_spec(d), full(ln_g), full(ln_b), full(w_qkv), full(w_ft), full(b_f), full(triu)],
        out_specs=[row_spec(D_ATTN)] * 5 + [t_spec, t_spec],
        out_shape=[jax.ShapeDtypeStruct((rows, D_ATTN), F32)] * 3
        + [jax.ShapeDtypeStruct((rows, D_ATTN), BF16)] * 2
        + [jax.ShapeDtypeStruct((N_HEADS, rows), F32)] * 2,
        scratch_shapes=[pltpu.VMEM((N_HEADS, LANES), F32)],
        compiler_params=pltpu.CompilerParams(dimension_semantics=("arbitrary",),
                                             vmem_limit_bytes=VMEM_LIMIT),
        name="qkv",
    )(x, ln_g, ln_b, w_qkv, w_ft, b_f, triu)


def _pattn_kernel(q_ref, k_ref, v_ref, c_ref, o_ref, *, tq, tk):
    hp = pl.program_id(1)
    qi = pl.program_id(2)
    q = q_ref[...].astype(BF16)
    lane = lax.broadcasted_iota(I32, (tq, LANES), 1)
    row = lax.broadcasted_iota(I32, (tq, tk), 0)
    col = lax.broadcasted_iota(I32, (tq, tk), 1)
    zero = jnp.zeros_like(q)
    q_heads = (jnp.where(lane < HEAD_DIM, q, zero), jnp.where(lane >= HEAD_DIM, q, zero))
    sub = tq // tk

    def step(kj, carry, diag):
        ks = pl.multiple_of(kj * tk, tk)
        kt = k_ref[pl.ds(ks, tk), :]
        vt = v_ref[pl.ds(ks, tk), :]
        new = []
        for hh in range(2):
            m, l, acc = carry[hh]
            s = lax.dot_general(q_heads[hh], kt, NT, preferred_element_type=F32)
            s = s - c_ref[pl.ds(2 * hp + hh, 1), pl.ds(ks, tk)]
            if diag is not None:
                s = jnp.where(col + diag * tk <= row, s, -jnp.inf)
            m_new = jnp.maximum(m, jnp.max(s, axis=-1, keepdims=True))
            alpha = jnp.exp(m - m_new)
            p = jnp.exp(s - m_new)
            l = alpha * l + jnp.sum(p, axis=-1, keepdims=True)
            acc = alpha * acc + jnp.dot(p.astype(BF16), vt, preferred_element_type=F32)
            new.append((m_new, l, acc))
        return tuple(new)

    init = (jnp.full((tq, 1), -jnp.inf, F32), jnp.zeros((tq, 1), F32),
            jnp.zeros((tq, LANES), F32))
    carry = lax.fori_loop(0, qi * sub, functools.partial(step, diag=None), (init, init))
    for d in range(sub):
        carry = step(qi * sub + d, carry, d)
    (_, l0, acc0), (_, l1, acc1) = carry
    o_ref[...] = jnp.where(lane < HEAD_DIM, acc0 / l0, acc1 / l1).astype(BF16)


def _prompt_attention(q, kb, vb, c_t, batch, seq):
    tq, tk = ATTN_Q_TILE, ATTN_K_TILE
    nq = seq // tq
    return pl.pallas_call(
        functools.partial(_pattn_kernel, tq=tq, tk=tk),
        grid=(batch, D_ATTN // LANES, nq),
        in_specs=[
            pl.BlockSpec((tq, LANES), lambda b, h, i: (b * nq + i, h)),
            pl.BlockSpec((seq, LANES), lambda b, h, i: (b, h)),
            pl.BlockSpec((seq, LANES), lambda b, h, i: (b, h)),
            pl.BlockSpec((N_HEADS, seq), lambda b, h, i: (0, b)),
        ],
        out_specs=pl.BlockSpec((tq, LANES), lambda b, h, i: (b * nq + i, h)),
        out_shape=jax.ShapeDtypeStruct((batch * seq, D_ATTN), BF16),
        compiler_params=pltpu.CompilerParams(
            dimension_semantics=("arbitrary", "arbitrary", "arbitrary"),
            vmem_limit_bytes=VMEM_LIMIT),
        name="pattn",
    )(q, kb, vb, c_t)


def _sattn_kernel(pt_ref, q_ref, kn_ref, vn_ref, lfn_ref, triu_ref, pagesel_ref, *rest,
                  n_pages, page, t_new):
    del pt_ref
    k_refs = rest[:n_pages]
    v_refs = rest[n_pages:2 * n_pages]
    f_refs = rest[2 * n_pages:3 * n_pages]
    o_ref = rest[3 * n_pages]
    nrow = t_new * N_HEADS
    q = q_ref[0]
    qrep = jnp.concatenate([jnp.broadcast_to(q[t:t + 1], (N_HEADS, D_ATTN)) for t in range(t_new)],
                           axis=0)
    rowi = lax.broadcasted_iota(I32, (nrow, D_ATTN), 0)
    lanei = lax.broadcasted_iota(I32, (nrow, D_ATTN), 1)
    head_mask = (lanei // HEAD_DIM) == (rowi % N_HEADS)
    qexp = jnp.where(head_mask, qrep, 0.0).astype(BF16)
    qexp_f = qexp.astype(F32)

    lf = jnp.concatenate([f_refs[p][0] for p in range(n_pages)], axis=0)
    c_in = _dot_sel(lf, triu_ref[...])
    tot = jnp.broadcast_to(c_in[:, page - 1:page], (n_pages * N_HEADS, page))
    c_all = c_in + _sel_dot(pagesel_ref[...], tot)
    c_tot = c_all[(n_pages - 1) * N_HEADS:, page - 1:page]

    logits = []
    for p in range(n_pages):
        kp = k_refs[p][0].astype(BF16)
        s = lax.dot_general(qexp, kp, NT, preferred_element_type=F32)
        cp = c_all[p * N_HEADS:(p + 1) * N_HEADS]
        logits.append(s - jnp.concatenate([cp] * t_new, axis=0))
    m = logits[0]
    for s in logits[1:]:
        m = jnp.maximum(m, s)
    m = jnp.max(m, axis=-1, keepdims=True)

    kn = kn_ref[0].astype(BF16).astype(F32)
    vn = vn_ref[0].astype(BF16).astype(F32)
    lfn = lfn_ref[0]
    trow = lax.broadcasted_iota(I32, (nrow, 1), 0) // N_HEADS
    new_logits = []
    cn = c_tot
    for j in range(t_new):
        cn = cn + lfn[:, j:j + 1]
        sj = jnp.sum(qexp_f * kn[j:j + 1], axis=-1, keepdims=True)
        sj = sj - jnp.concatenate([cn] * t_new, axis=0)
        sj = jnp.where(trow >= j, sj, -jnp.inf)
        new_logits.append(sj)
        m = jnp.maximum(m, sj)

    l = jnp.zeros((nrow, 1), F32)
    acc = jnp.zeros((nrow, D_ATTN), F32)
    for p in range(n_pages):
        pr = jnp.exp(logits[p] - m)
        l = l + jnp.sum(pr, axis=-1, keepdims=True)
        acc = acc + jnp.dot(pr.astype(BF16), v_refs[p][0].astype(BF16), preferred_element_type=F32)
    for j in range(t_new):
        pj = jnp.exp(new_logits[j] - m)
        l = l + pj
        acc = acc + pj.astype(BF16).astype(F32) * vn[j:j + 1]
    acc = jnp.where(head_mask, acc / l, 0.0)
    o_ref[0] = jnp.concatenate(
        [jnp.sum(acc[t * N_HEADS:(t + 1) * N_HEADS], axis=0, keepdims=True) for t in range(t_new)], axis=0)


def _sample_attention(page_table, q_s, k_s, v_s, lf_s, cache_k, cache_v, cache_lft):
    n_seq, n_pages = page_table.shape
    page = cache_k.shape[1]
    t_new = q_s.shape[1]
    triu = jnp.triu(jnp.ones((page, page), F32)).astype(BF16)
    r = jnp.arange(n_pages * N_HEADS)
    pagesel = ((r[:, None] % N_HEADS == r[None, :] % N_HEADS)
               & (r[None, :] // N_HEADS < r[:, None] // N_HEADS)).astype(BF16)

    def page_spec(p, shape):
        return pl.BlockSpec((1,) + shape, lambda n, pt, p=p: (pt[n * n_pages + p], 0, 0))

    seq_spec = lambda a: pl.BlockSpec((1,) + a.shape[1:], lambda n, pt: (n, 0, 0))
    const = lambda a: pl.BlockSpec(a.shape, lambda n, pt: (0, 0))
    grid_spec = pltpu.PrefetchScalarGridSpec(
        num_scalar_prefetch=1,
        grid=(n_seq,),
        in_specs=[seq_spec(q_s), seq_spec(k_s), seq_spec(v_s), seq_spec(lf_s), const(triu), const(pagesel)]
        + [page_spec(p, (page, D_ATTN)) for p in range(n_pages)]
        + [page_spec(p, (page, D_ATTN)) for p in range(n_pages)]
        + [page_spec(p, (N_HEADS, page)) for p in range(n_pages)],
        out_specs=pl.BlockSpec((1, t_new, D_ATTN), lambda n, pt: (n, 0, 0)),
    )
    return pl.pallas_call(
        functools.partial(_sattn_kernel, n_pages=n_pages, page=page, t_new=t_new),
        grid_spec=grid_spec,
        out_shape=jax.ShapeDtypeStruct((n_seq, t_new, D_ATTN), F32),
        compiler_params=pltpu.CompilerParams(dimension_semantics=("arbitrary",),
                                             vmem_limit_bytes=VMEM_LIMIT),
        name="sattn",
    )(page_table.reshape(-1), q_s, k_s, v_s, lf_s, triu, pagesel,
      *([cache_k] * n_pages), *([cache_v] * n_pages), *([cache_lft] * n_pages))


def _sattn_native_kernel(pt_ref, q_ref, kn_ref, vn_ref, lfn_ref, scan_ref, last_ref, lower_ref, *rest,
                         n_pages, page, t_new):
    del pt_ref
    k_refs = rest[:n_pages]
    v_refs = rest[n_pages:2 * n_pages]
    f_refs = rest[2 * n_pages:3 * n_pages]
    o_ref = rest[3 * n_pages]
    nrow = t_new * N_HEADS
    width = page * N_HEADS
    chunks = width // LANES
    q = q_ref[0].astype(BF16)
    rowi = lax.broadcasted_iota(I32, (nrow, width), 0)
    lanei = lax.broadcasted_iota(I32, (nrow, width), 1)
    same_head = (lanei % N_HEADS) == (rowi % N_HEADS)

    lf = jnp.concatenate([f_refs[p][0] for p in range(n_pages)], axis=0)
    local = _dot_sel(lf, scan_ref[...])
    tot = _dot_sel(local, last_ref[...])
    before = _sel_dot(lower_ref[...], tot)
    c_all = local + before
    c_tot = before[n_pages * chunks - 1:] + tot[n_pages * chunks - 1:]

    logits = []
    for p in range(n_pages):
        s = lax.dot_general(q, k_refs[p][0].astype(BF16), NT, preferred_element_type=F32)
        sc = [s[:, j * LANES:(j + 1) * LANES] - c_all[p * chunks + j:p * chunks + j + 1]
              for j in range(chunks)]
        logits.append(jnp.where(same_head, jnp.concatenate(sc, axis=1), -jnp.inf))
    m = logits[0]
    for s in logits[1:]:
        m = jnp.maximum(m, s)
    m = jnp.max(m, axis=-1, keepdims=True)

    pad = jnp.zeros((LANES - nrow, HEAD_DIM), F32)
    kn = jnp.concatenate([kn_ref[0], pad], axis=0).astype(BF16)
    vn = jnp.concatenate([vn_ref[0], pad], axis=0).astype(BF16)
    cn = c_tot + _dot_sel(lfn_ref[0], scan_ref[...])[0:1]
    sn = lax.dot_general(q, kn, NT, preferred_element_type=F32) - cn
    rown = lax.broadcasted_iota(I32, (nrow, LANES), 0)
    lanen = lax.broadcasted_iota(I32, (nrow, LANES), 1)
    visible = jnp.logical_and(lanen < nrow, lanen // N_HEADS <= rown // N_HEADS)
    visible = jnp.logical_and(visible, (lanen % N_HEADS) == (rown % N_HEADS))
    sn = jnp.where(visible, sn, -jnp.inf)
    m = jnp.maximum(m, jnp.max(sn, axis=-1, keepdims=True))

    pn = jnp.exp(sn - m)
    l = jnp.sum(pn, axis=-1, keepdims=True)
    acc = jnp.dot(pn.astype(BF16), vn, preferred_element_type=F32)
    for p in range(n_pages):
        pr = jnp.exp(logits[p] - m)
        l = l + jnp.sum(pr, axis=-1, keepdims=True)
        acc = acc + jnp.dot(pr.astype(BF16), v_refs[p][0].astype(BF16), preferred_element_type=F32)
    o_ref[0] = acc / l


def _sample_attention_native(page_table, q_s, k_s, v_s, lf_s, cache_k, cache_v, cache_lf):
    n_seq, n_pages = page_table.shape
    width = cache_k.shape[1]
    page = width // N_HEADS
    chunks = width // LANES
    nrow = q_s.shape[1]
    r = jnp.arange(LANES)
    same = r[:, None] % N_HEADS == r[None, :] % N_HEADS
    scan = (same & (r[:, None] // N_HEADS <= r[None, :] // N_HEADS)).astype(BF16)
    last = (same & (r[:, None] // N_HEADS == LANES // N_HEADS - 1)).astype(BF16)
    lower = jnp.tril(jnp.ones((n_pages * chunks, n_pages * chunks), F32), -1).astype(BF16)

    def page_spec(p, shape):
        return pl.BlockSpec((1,) + shape, lambda n, pt, p=p: (pt[n * n_pages + p], 0, 0))

    seq_spec = lambda a: pl.BlockSpec((1,) + a.shape[1:], lambda n, pt: (n, 0, 0))
    const = lambda a: pl.BlockSpec(a.shape, lambda n, pt: (0, 0))
    grid_spec = pltpu.PrefetchScalarGridSpec(
        num_scalar_prefetch=1,
        grid=(n_seq,),
        in_specs=[seq_spec(q_s), seq_spec(k_s), seq_spec(v_s), seq_spec(lf_s), const(scan), const(last),
                  const(lower)]
        + [page_spec(p, (width, HEAD_DIM)) for p in range(n_pages)]
        + [page_spec(p, (width, HEAD_DIM)) for p in range(n_pages)]
        + [page_spec(p, (chunks, LANES)) for p in range(n_pages)],
        out_specs=pl.BlockSpec((1, nrow, HEAD_DIM), lambda n, pt: (n, 0, 0)),
    )
    return pl.pallas_call(
        functools.partial(_sattn_native_kernel, n_pages=n_pages, page=page, t_new=nrow // N_HEADS),
        grid_spec=grid_spec,
        out_shape=jax.ShapeDtypeStruct((n_seq, nrow, HEAD_DIM), F32),
        compiler_params=pltpu.CompilerParams(dimension_semantics=("arbitrary",),
                                             vmem_limit_bytes=VMEM_LIMIT),
        name="sattn",
    )(page_table.reshape(-1), q_s, k_s, v_s, lf_s, scan, last, lower,
      *([cache_k] * n_pages), *([cache_v] * n_pages), *([cache_lf] * n_pages))


def _mixer_kernel(x_ref, yap_ref, yas_ref, tap0_ref, tap1_ref, lng_ref, lnb_ref, wc_ref, wg_ref,
                  cw_ref, wbc_ref, wba_ref, wo_ref, l1g_ref, l1b_ref, wr_ref, br_ref, tril_ref,
                  x1_ref, route_ref, cnt_ref, ulast_ref, us_ref,
                  ext_ref, cnt_acc_ref, *, tiles_per_seq, n_prompt_tiles, t_new, alpha):
    i = pl.program_id(0)
    tm = x_ref.shape[0]
    d_conv = us_ref.shape[1]
    is_sample = i >= n_prompt_tiles
    xn = _ln(x_ref[...], lng_ref[...], lnb_ref[...])
    xb = xn.astype(BF16)

    zc = jnp.dot(xb, wc_ref[...], preferred_element_type=F32)
    u = zc[:, d_conv:2 * d_conv] * zc[:, 2 * d_conv:]

    @pl.when(i % tiles_per_seq == 0)
    def _():
        ext_ref[0:SUBLANES, :] = jnp.zeros((SUBLANES, d_conv), F32)

    ext_ref[SUBLANES:, :] = u
    prev1 = ext_ref[SUBLANES - 1:SUBLANES - 1 + tm, :]
    prev2 = ext_ref[SUBLANES - 2:SUBLANES - 2 + tm, :]
    ext_ref[0:SUBLANES, :] = u[tm - SUBLANES:]
    t_in_seq = lax.broadcasted_iota(I32, (tm, 1), 0) % t_new
    keep1 = jnp.logical_or(jnp.logical_not(is_sample), t_in_seq >= 1)
    keep2 = jnp.logical_or(jnp.logical_not(is_sample), t_in_seq >= 2)
    prev1 = jnp.where(keep1, prev1, tap1_ref[...])
    prev2 = jnp.where(keep2, prev2, tap0_ref[...])
    cw = cw_ref[...]
    y_conv = zc[:, :d_conv] * (prev2 * cw[0:1] + prev1 * cw[1:2] + u * cw[2:3])
    ulast_ref[0] = u[tm - SUBLANES:]
    us_ref[...] = u

    ya = jnp.where(is_sample, yas_ref[...].astype(BF16), yap_ref[...])
    bc = jnp.dot(y_conv.astype(BF16), wbc_ref[...], preferred_element_type=F32)
    ba = jnp.dot(ya, wba_ref[...], preferred_element_type=F32)
    g = jnp.dot(xb, wg_ref[...], preferred_element_type=F32)
    d_model = bc.shape[1]
    merged = _sigmoid(g[:, :d_model]) * bc + _sigmoid(g[:, d_model:]) * ba
    mix = jnp.dot(merged.astype(BF16), wo_ref[...], preferred_element_type=F32)
    x1 = _ln(alpha * xn + mix, l1g_ref[...], l1b_ref[...])
    x1_ref[...] = x1

    logits = jnp.dot(x1, wr_ref[...], preferred_element_type=F32,
                     precision=lax.Precision.HIGHEST) + br_ref[...]
    lane = lax.broadcasted_iota(I32, (tm, LANES), 1)
    lane_f = lane.astype(F32)
    logits = jnp.where(lane < N_EXPERTS, logits, -jnp.inf)
    vals, hots = [], []
    for _ in range(TOP_K):
        mx = jnp.max(logits, axis=-1, keepdims=True)
        idx = jnp.min(jnp.where(logits == mx, lane_f, float(LANES)), axis=-1, keepdims=True)
        hot = lane_f == idx
        logits = jnp.where(hot, -jnp.inf, logits)
        vals.append(mx)
        hots.append(hot)
    exps = [jnp.exp(v - vals[0]) for v in vals]
    denom = exps[0]
    for e in exps[1:]:
        denom = denom + e

    @pl.when(i == 0)
    def _():
        cnt_acc_ref[...] = jnp.zeros_like(cnt_acc_ref)

    chosen = hots[0]
    for h in hots[1:]:
        chosen = jnp.logical_or(chosen, h)
    chosen_f = jnp.where(chosen, 1.0, 0.0)
    before = jnp.dot(tril_ref[...], chosen_f.astype(BF16), preferred_element_type=F32) + cnt_acc_ref[0:1, :]
    route = jnp.zeros((tm, LANES), F32)
    for kk in range(TOP_K):
        idx_f = jnp.sum(jnp.where(hots[kk], lane_f, 0.0), axis=-1, keepdims=True)
        rank = jnp.sum(jnp.where(hots[kk], before, 0.0), axis=-1, keepdims=True)
        route = jnp.where(lane == kk, idx_f, route)
        route = jnp.where(lane == TOP_K + kk, rank, route)
        route = jnp.where(lane == 2 * TOP_K + kk, exps[kk] / denom, route)
    route_ref[...] = route
    cnt = cnt_acc_ref[0:1, :] + jnp.sum(chosen_f, axis=0, keepdims=True)
    cnt_acc_ref[...] = jnp.broadcast_to(cnt, cnt_acc_ref.shape)
    cnt_ref[...] = jnp.broadcast_to(cnt, cnt_ref.shape)


def _mixer(x, ya_p, ya_s, tap0, tap1, ln_g, ln_b, w_c, w_g, conv_w, w_bc, w_ba, w_o, l1g, l1b,
           w_r, b_r, seq, n_prompt_rows, t_new, alpha):
    rows, d = x.shape
    tm = ROW_TILE
    n_prompt_tiles = n_prompt_rows // tm
    n_tiles = rows // tm
    d_conv = w_bc.shape[0]
    tril = jnp.tril(jnp.ones((tm, tm), F32), -1).astype(BF16)
    row_spec = lambda w: pl.BlockSpec((tm, w), lambda i: (i, 0))
    prompt_spec = lambda w: pl.BlockSpec((tm, w), lambda i: (jnp.minimum(i, n_prompt_tiles - 1), 0))
    sample_spec = lambda w: pl.BlockSpec((tm, w), lambda i: (jnp.maximum(i - n_prompt_tiles, 0), 0))
    full = lambda a: pl.BlockSpec(a.shape, lambda i: (0,) * a.ndim)
    weights = [ln_g, ln_b, w_c, w_g, conv_w, w_bc, w_ba, w_o, l1g, l1b, w_r, b_r, tril]
    n_sample_rows = rows - n_prompt_rows
    return pl.pallas_call(
        functools.partial(_mixer_kernel, tiles_per_seq=seq // tm, n_prompt_tiles=n_prompt_tiles,
                          t_new=t_new, alpha=alpha),
        grid=(n_tiles,),
        in_specs=[row_spec(d), prompt_spec(D_ATTN), sample_spec(D_ATTN), sample_spec(d_conv),
                  sample_spec(d_conv)] + [full(w) for w in weights],
        out_specs=[row_spec(d), row_spec(LANES), pl.BlockSpec((SUBLANES, LANES), lambda i: (0, 0)),
                   pl.BlockSpec((1, SUBLANES, d_conv), lambda i: (i, 0, 0)), sample_spec(d_conv)],
        out_shape=[jax.ShapeDtypeStruct((rows, d), F32), jax.ShapeDtypeStruct((rows, LANES), F32),
                   jax.ShapeDtypeStruct((SUBLANES, LANES), F32),
                   jax.ShapeDtypeStruct((n_tiles, SUBLANES, d_conv), F32),
                   jax.ShapeDtypeStruct((n_sample_rows, d_conv), F32)],
        scratch_shapes=[pltpu.VMEM((tm + SUBLANES, d_conv), F32), pltpu.VMEM((SUBLANES, LANES), F32)],
        compiler_params=pltpu.CompilerParams(dimension_semantics=("arbitrary",),
                                             vmem_limit_bytes=VMEM_LIMIT),
        name="mixer",
    )(x, ya_p, ya_s, tap0, tap1, *weights)


def _dispatch_kernel(meta_ref, dest_ref, x_ref, xs_ref, zero_ref, sem_ref, *, tile, block, n_blocks):
    i = pl.program_id(0)

    def row_copy(j, kk):
        return pltpu.make_async_copy(x_ref.at[pl.ds(j, 1), :],
                                     xs_ref.at[pl.ds(dest_ref[j * TOP_K + kk], 1), :], sem_ref.at[0])

    def zero_copy(start):
        return pltpu.make_async_copy(zero_ref, xs_ref.at[pl.ds(pl.multiple_of(start, block), block), :],
                                     sem_ref.at[1])

    def pad_copy(e):
        return zero_copy(meta_ref[e] - block)

    @pl.when(i == 0)
    def _():
        zero_ref[...] = jnp.zeros_like(zero_ref)
        n_used = meta_ref[2 * N_EXPERTS]
        for e in range(N_EXPERTS):
            @pl.when(meta_ref[N_EXPERTS + e] > 0)
            def _():
                pad_copy(e).start()

        def tail_start(b, carry):
            zero_copy(b * block).start()
            return carry

        def tail_wait(b, carry):
            zero_copy(b * block).wait()
            return carry

        lax.fori_loop(n_used, n_blocks, tail_start, 0)
        for e in range(N_EXPERTS):
            @pl.when(meta_ref[N_EXPERTS + e] > 0)
            def _():
                pad_copy(e).wait()
        lax.fori_loop(n_used, n_blocks, tail_wait, 0)

    def issue(j, carry):
        for kk in range(TOP_K):
            row_copy(j, kk).start()
        return carry

    def drain(j, carry):
        for kk in range(TOP_K):
            row_copy(j, kk).wait()
        return carry

    lax.fori_loop(0, tile, issue, 0)
    lax.fori_loop(0, tile, drain, 0)


def _dispatch(meta, dest, x1, n_rows_padded):
    rows, d = x1.shape
    tile = MOE_TILE
    grid_spec = pltpu.PrefetchScalarGridSpec(
        num_scalar_prefetch=1,
        grid=(rows // tile,),
        in_specs=[pl.BlockSpec((tile * TOP_K,), lambda i, m: (i,), memory_space=pltpu.SMEM),
                  pl.BlockSpec((tile, d), lambda i, m: (i, 0))],
        out_specs=pl.BlockSpec(memory_space=pl.ANY),
        scratch_shapes=[pltpu.VMEM((MOE_BLOCK, d), F32), pltpu.SemaphoreType.DMA((2,))],
    )
    return pl.pallas_call(
        functools.partial(_dispatch_kernel, tile=tile, block=MOE_BLOCK,
                          n_blocks=n_rows_padded // MOE_BLOCK),
        grid_spec=grid_spec,
        out_shape=jax.ShapeDtypeStruct((n_rows_padded, d), F32),
        compiler_params=pltpu.CompilerParams(dimension_semantics=("arbitrary",),
                                             vmem_limit_bytes=VMEM_LIMIT),
        name="dispatch",
    )(meta, dest, x1)


def _experts_kernel(be_ref, nb_ref, xs_ref, wg_ref, bg_ref, wu_ref, bu_ref, wd_ref, bd_ref, o_ref,
                    wgb_ref, wub_ref, wdb_ref):
    b = pl.program_id(0)
    prev = be_ref[jnp.maximum(b - 1, 0)]

    @pl.when(jnp.logical_or(b == 0, be_ref[b] != prev))
    def _():
        wgb_ref[...] = wg_ref[0].astype(BF16)
        wub_ref[...] = wu_ref[0].astype(BF16)
        wdb_ref[...] = wd_ref[0].astype(BF16)

    @pl.when(b < nb_ref[0])
    def _():
        xb = xs_ref[...].astype(BF16)
        g = jnp.dot(xb, wgb_ref[...], preferred_element_type=F32) + bg_ref[0]
        u = jnp.dot(xb, wub_ref[...], preferred_element_type=F32) + bu_ref[0]
        g = jnp.minimum(g, SWIGLU_LIMIT)
        u = jnp.clip(u, -SWIGLU_LIMIT, SWIGLU_LIMIT)
        h = (u + 1.0) * (g * _sigmoid(SWIGLU_ALPHA * g))
        o_ref[...] = jnp.dot(h.astype(BF16), wdb_ref[...], preferred_element_type=F32) + bd_ref[0]

    @pl.when(b >= nb_ref[0])
    def _():
        o_ref[...] = jnp.zeros_like(o_ref)


def _experts(block_e, n_used, xs, w_gate, b_gate, w_up, b_up, w_down, b_down):
    n_rows, d = xs.shape
    bm = MOE_BLOCK
    d_e = w_gate.shape[2]
    x_spec = pl.BlockSpec((bm, d), lambda b, be, nb: (b, 0))
    w_spec = lambda a: pl.BlockSpec((1,) + a.shape[1:], lambda b, be, nb: (be[b], 0, 0))
    grid_spec = pltpu.PrefetchScalarGridSpec(
        num_scalar_prefetch=2,
        grid=(n_rows // bm,),
        in_specs=[x_spec, w_spec(w_gate), w_spec(b_gate), w_spec(w_up), w_spec(b_up), w_spec(w_down),
                  w_spec(b_down)],
        out_specs=x_spec,
        scratch_shapes=[pltpu.VMEM((d, d_e), BF16), pltpu.VMEM((d, d_e), BF16), pltpu.VMEM((d_e, d), BF16)],
    )
    return pl.pallas_call(
        _experts_kernel,
        grid_spec=grid_spec,
        out_shape=jax.ShapeDtypeStruct((n_rows, d), F32),
        compiler_params=pltpu.CompilerParams(dimension_semantics=("arbitrary",),
                                             vmem_limit_bytes=VMEM_LIMIT),
        name="experts",
    )(block_e, n_used, xs, w_gate, b_gate, w_up, b_up, w_down, b_down)


def _combine_kernel(dest_ref, x1_ref, route_ref, pp_ref, ps_ref, ys_ref, l2g_ref, l2b_ref, wpg_ref, wpp_ref,
                    y_ref, gbuf_ref, sem_ref, *, tile, n_prompt_tiles, alpha):
    i = pl.program_id(0)

    def row_copy(j, kk):
        return pltpu.make_async_copy(ys_ref.at[pl.ds(dest_ref[j * TOP_K + kk], 1), :],
                                     gbuf_ref.at[kk, pl.ds(j, 1), :], sem_ref.at[0])

    def issue(j, carry):
        for kk in range(TOP_K):
            row_copy(j, kk).start()
        return carry

    def drain(j, carry):
        for kk in range(TOP_K):
            row_copy(j, kk).wait()
        return carry

    lax.fori_loop(0, tile, issue, 0)
    lax.fori_loop(0, tile, drain, 0)

    route = route_ref[...]
    ffn = route[:, 2 * TOP_K:2 * TOP_K + 1] * gbuf_ref[0]
    for kk in range(1, TOP_K):
        ffn = ffn + route[:, 2 * TOP_K + kk:2 * TOP_K + kk + 1] * gbuf_ref[kk]
    x2 = _ln(alpha * x1_ref[...] + ffn, l2g_ref[...], l2b_ref[...])
    p = jnp.where(i >= n_prompt_tiles, ps_ref[...], pp_ref[...]).astype(BF16)
    gate = _sigmoid(jnp.dot(x2.astype(BF16), wpg_ref[...], preferred_element_type=F32))
    y_ref[...] = x2 + gate * jnp.dot(p, wpp_ref[...], preferred_element_type=F32)


def _combine(dest, x1, route, p_p, p_s, ys, l2g, l2b, w_pg, w_pp, n_prompt_rows, alpha):
    rows, d = x1.shape
    tile = MOE_TILE
    n_prompt_tiles = n_prompt_rows // tile
    d_ple = p_p.shape[1]
    row_spec = lambda w: pl.BlockSpec((tile, w), lambda i: (i, 0))
    full = lambda a: pl.BlockSpec(a.shape, lambda i: (0,) * a.ndim)
    return pl.pallas_call(
        functools.partial(_combine_kernel, tile=tile, n_prompt_tiles=n_prompt_tiles, alpha=alpha),
        grid=(rows // tile,),
        in_specs=[pl.BlockSpec((tile * TOP_K,), lambda i: (i,), memory_space=pltpu.SMEM),
                  row_spec(d), row_spec(LANES),
                  pl.BlockSpec((tile, d_ple), lambda i: (jnp.minimum(i, n_prompt_tiles - 1), 0)),
                  pl.BlockSpec((tile, d_ple), lambda i: (jnp.maximum(i - n_prompt_tiles, 0), 0)),
                  pl.BlockSpec(memory_space=pl.ANY), full(l2g), full(l2b), full(w_pg), full(w_pp)],
        out_specs=row_spec(d),
        out_shape=jax.ShapeDtypeStruct((rows, d), F32),
        scratch_shapes=[pltpu.VMEM((TOP_K, tile, d), F32), pltpu.SemaphoreType.DMA((1,))],
        compiler_params=pltpu.CompilerParams(dimension_semantics=("arbitrary",),
                                             vmem_limit_bytes=VMEM_LIMIT),
        name="combine",
    )(dest, x1, route, p_p, p_s, ys, l2g, l2b, w_pg, w_pp)


def kernel(x_prompt, x_sample, cache_k, cache_v, cache_logf, state_conv, page_table, p_prompt, p_sample,
           ln_in_g, ln_in_b, w_in, b_f, conv_w, w_br_conv, w_br_attn, w_o, ln1_g, ln1_b, w_router,
           b_router, w_gate, b_gate, w_up, b_up, w_down, b_down, ln2_g, ln2_b, w_ple_gate, w_ple_proj):
    depth = w_in.shape[0]
    assert depth == 1, "single-layer step only"
    batch, seq, d = x_prompt.shape
    n_seq, t_new, _ = x_sample.shape
    d_conv = conv_w.shape[2]
    n_pool, page = cache_k.shape[1], cache_k.shape[2]
    alpha = (2 * depth) ** 0.25
    n_prompt_rows = batch * seq
    n_sample_rows = n_seq * t_new
    rows = n_prompt_rows + n_sample_rows
    assert seq % ROW_TILE == 0 and n_sample_rows % ROW_TILE == 0 and seq % ATTN_Q_TILE == 0
    assert ATTN_Q_TILE % ATTN_K_TILE == 0
    assert t_new >= CONV_TAPS - 1

    row2 = lambda a: a.reshape(1, -1)
    x = jnp.concatenate([x_prompt.reshape(n_prompt_rows, d), x_sample.reshape(n_sample_rows, d)], axis=0)

    w = w_in[0]
    o_qkv = 3 * d_conv
    o_f = o_qkv + 3 * D_ATTN
    o_g = o_f + N_HEADS
    w_c = w[:, :o_qkv].astype(BF16)
    w_qkv = w[:, o_qkv:o_f].astype(BF16)
    w_ft = jnp.zeros((2 * SUBLANES, d), BF16).at[:N_HEADS].set(w[:, o_f:o_g].T.astype(BF16))
    w_g = w[:, o_g:].astype(BF16)

    q, k, v, kb, vb, lf_t, c_t = _qkv(x, row2(ln_in_g), row2(ln_in_b), w_qkv, w_ft,
                                      b_f[0].reshape(N_HEADS, 1), seq)

    ya_p = _prompt_attention(q, kb, vb, c_t, batch, seq)

    sl = lambda a: a[n_prompt_rows:].reshape(n_seq, t_new * N_HEADS, HEAD_DIM)
    lf_new = lf_t[:, n_prompt_rows:].T.reshape(n_seq, 1, t_new * N_HEADS)
    lf_s = jnp.pad(lf_new, ((0, 0), (0, SUBLANES - 1), (0, LANES - t_new * N_HEADS)))
    ya_s = _sample_attention_native(page_table, sl(q), sl(k), sl(v), lf_s,
                                    cache_k[0].reshape(n_pool, page * N_HEADS, HEAD_DIM),
                                    cache_v[0].reshape(n_pool, page * N_HEADS, HEAD_DIM),
                                    cache_logf[0].reshape(n_pool, page * N_HEADS // LANES, LANES))

    st = state_conv[0]
    zeros = jnp.zeros((n_seq, t_new - 2, d_conv), F32)
    tap0 = jnp.concatenate([st, zeros], axis=1).reshape(n_sample_rows, d_conv)
    tap1 = jnp.concatenate([st[:, 1:], jnp.zeros((n_seq, t_new - 1, d_conv), F32)],
                           axis=1).reshape(n_sample_rows, d_conv)

    w_r = jnp.zeros((d, LANES), F32).at[:, :N_EXPERTS].set(w_router[0])
    b_r = jnp.zeros((1, LANES), F32).at[:, :N_EXPERTS].set(b_router[0])
    conv_w8 = jnp.zeros((SUBLANES, d_conv), F32).at[:CONV_TAPS].set(conv_w[0])
    x1, route, counts, u_last, u_s = _mixer(
        x, ya_p, ya_s.reshape(n_sample_rows, D_ATTN), tap0, tap1, row2(ln_in_g), row2(ln_in_b), w_c, w_g,
        conv_w8, w_br_conv[0].astype(BF16), w_br_attn[0].astype(BF16), w_o[0].astype(BF16),
        row2(ln1_g[0]), row2(ln1_b[0]), w_r, b_r, seq, n_prompt_rows, t_new, alpha)

    bm = MOE_BLOCK
    n_assign = rows * TOP_K
    n_blocks = -(-n_assign // bm) + N_EXPERTS
    cnt = counts[0, :N_EXPERTS].astype(I32)
    padded = (cnt + bm - 1) // bm * bm
    pad_end = jnp.cumsum(padded)
    pad_start = pad_end - padded
    idx = route[:, :TOP_K].astype(I32)
    rank = route[:, TOP_K:2 * TOP_K].astype(I32)
    dest = (pad_start[idx] + rank).reshape(-1)
    n_used = (pad_end[-1] // bm).astype(I32).reshape(1)
    blk = jnp.minimum(jnp.arange(n_blocks, dtype=I32), n_used[0] - 1) * bm
    block_e = jnp.minimum(jnp.sum(blk[:, None] >= pad_end[None, :], axis=1), N_EXPERTS - 1).astype(I32)
    meta = jnp.concatenate([pad_end, padded, n_used]).astype(I32)

    xs = _dispatch(meta, dest, x1, n_blocks * bm)
    as3 = lambda a: a[0].reshape(N_EXPERTS, 1, -1)
    ys = _experts(block_e, n_used, xs, w_gate[0], as3(b_gate), w_up[0], as3(b_up), w_down[0], as3(b_down))
    y = _combine(dest, x1, route, p_prompt[0].reshape(n_prompt_rows, -1),
                 p_sample[0].reshape(n_sample_rows, -1), ys, row2(ln2_g[0]), row2(ln2_b[0]),
                 w_ple_gate[0].astype(BF16), w_ple_proj[0].astype(BF16), n_prompt_rows, alpha)

    tiles_per_seq = seq // ROW_TILE
    conv_p = u_last[tiles_per_seq - 1::tiles_per_seq][:batch, SUBLANES - 2:]
    conv_s = u_s.reshape(n_seq, t_new, d_conv)[:, t_new - 2:]
    pr = lambda a, *s: a[:n_prompt_rows].reshape((1, batch, seq) + s)
    sr = lambda a, *s: a[n_prompt_rows:].reshape((1, n_seq, t_new) + s)
    lf = lf_t.T
    return (y[:n_prompt_rows].reshape(batch, seq, d), y[n_prompt_rows:].reshape(n_seq, t_new, d),
            pr(k, N_HEADS, HEAD_DIM), pr(v, N_HEADS, HEAD_DIM), pr(lf, N_HEADS), conv_p[None],
            sr(k, N_HEADS, HEAD_DIM), sr(v, N_HEADS, HEAD_DIM), sr(lf, N_HEADS), conv_s[None])
```

```python
import functools

import jax
import jax.numpy as jnp
from jax import lax
from jax.experimental import pallas as pl
from jax.experimental.pallas import tpu as pltpu

F32 = jnp.float32
BF16 = jnp.bfloat16
I32 = jnp.int32

N_HEADS = 8
HEAD_DIM = 64
D_ATTN = N_HEADS * HEAD_DIM
TOP_K = 4
CONV_TAPS = 3
N_EXPERTS = 32
LN_EPS = 1e-5
SWIGLU_LIMIT = 7.0
SWIGLU_ALPHA = 1.702
LANES = 128
SUBLANES = 8
VMEM_LIMIT = 56 * 1024 * 1024

NN = (((1,), (0,)), ((), ()))
NT = (((1,), (1,)), ((), ()))

ROW_TILE = 512
ATTN_Q_TILE = 512
ATTN_K_TILE = 512
MOE_BLOCK = 256
MOE_TILE = 256


def _ln(x, g, b):
    mu = jnp.mean(x, axis=-1, keepdims=True)
    xc = x - mu
    var = jnp.mean(xc * xc, axis=-1, keepdims=True)
    return xc * lax.rsqrt(var + LN_EPS) * g + b


def _log_sigmoid(x):
    return jnp.minimum(x, 0.0) - jnp.log1p(jnp.exp(-jnp.abs(x)))


def _sigmoid(x):
    return 1.0 / (1.0 + jnp.exp(-x))


def _split3(a):
    hi = a.astype(BF16)
    r = a - hi.astype(F32)
    mid = r.astype(BF16)
    lo = (r - mid.astype(F32)).astype(BF16)
    return hi, mid, lo


def _dot_sel(a, sel, dims=NN):
    out = None
    for piece in _split3(a):
        t = lax.dot_general(piece, sel, dims, preferred_element_type=F32)
        out = t if out is None else out + t
    return out


def _sel_dot(sel, a):
    out = None
    for piece in _split3(a):
        t = lax.dot_general(sel, piece, NN, preferred_element_type=F32)
        out = t if out is None else out + t
    return out


def _qkv_kernel(x_ref, g_ref, b_ref, wqkv_ref, wft_ref, bf_ref, triu_ref,
                q_ref, k_ref, v_ref, kb_ref, vb_ref, lf_ref, c_ref, carry_ref, *, tiles_per_seq):
    i = pl.program_id(0)
    xb = _ln(x_ref[...], g_ref[...], b_ref[...]).astype(BF16)
    z = jnp.dot(xb, wqkv_ref[...], preferred_element_type=F32)
    q_ref[...] = z[:, :D_ATTN] * (HEAD_DIM ** -0.5)
    k = z[:, D_ATTN:2 * D_ATTN]
    v = z[:, 2 * D_ATTN:]
    k_ref[...] = k
    v_ref[...] = v
    kb_ref[...] = k.astype(BF16)
    vb_ref[...] = v.astype(BF16)
    ft = lax.dot_general(wft_ref[...], xb, NT, preferred_element_type=F32)[:N_HEADS]
    lf = _log_sigmoid(ft + bf_ref[...])
    lf_ref[...] = lf

    @pl.when(i % tiles_per_seq == 0)
    def _():
        carry_ref[...] = jnp.zeros_like(carry_ref)

    c = _dot_sel(lf, triu_ref[...]) + carry_ref[:, :1]
    c_ref[...] = c
    carry_ref[...] = jnp.broadcast_to(c[:, -1:], carry_ref.shape)


def _qkv(x, ln_g, ln_b, w_qkv, w_ft, b_f, seq):
    rows, d = x.shape
    tm = ROW_TILE
    triu = jnp.triu(jnp.ones((tm, tm), F32)).astype(BF16)
    row_spec = lambda w: pl.BlockSpec((tm, w), lambda i: (i, 0))
    t_spec = pl.BlockSpec((N_HEADS, tm), lambda i: (0, i))
    full = lambda a: pl.BlockSpec(a.shape, lambda i: (0,) * a.ndim)
    return pl.pallas_call(
        functools.partial(_qkv_kernel, tiles_per_seq=seq // tm),
        grid=(rows // tm,),
        in_specs=[row_spec(d), full(ln_g), full(ln_b), full(w_qkv), full(w_ft), full(b_f), full(triu)],
        out_specs=[row_spec(D_ATTN)] * 5 + [t_spec, t_spec],
        out_shape=[jax.ShapeDtypeStruct((rows, D_ATTN), F32)] * 3
        + [jax.ShapeDtypeStruct((rows, D_ATTN), BF16)] * 2
        + [jax.ShapeDtypeStruct((N_HEADS, rows), F32)] * 2,
        scratch_shapes=[pltpu.VMEM((N_HEADS, LANES), F32)],
        compiler_params=pltpu.CompilerParams(dimension_semantics=("arbitrary",),
                                             vmem_limit_bytes=VMEM_LIMIT),
        name="qkv",
    )(x, ln_g, ln_b, w_qkv, w_ft, b_f, triu)


def _pattn_kernel(q_ref, k_ref, v_ref, c_ref, o_ref, *, tq, tk):
    hp = pl.program_id(1)
    qi = pl.program_id(2)
    q = q_ref[...].astype(BF16)
    lane = lax.broadcasted_iota(I32, (tq, LANES), 1)
    row = lax.broadcasted_iota(I32, (tq, tk), 0)
    col = lax.broadcasted_iota(I32, (tq, tk), 1)
    zero = jnp.zeros_like(q)
    q_heads = (jnp.where(lane < HEAD_DIM, q, zero), jnp.where(lane >= HEAD_DIM, q, zero))
    sub = tq // tk

    def step(kj, carry, diag):
        ks = pl.multiple_of(kj * tk, tk)
        kt = k_ref[pl.ds(ks, tk), :]
        vt = v_ref[pl.ds(ks, tk), :]
        new = []
        for hh in range(2):
            m, l, acc = carry[hh]
            s = lax.dot_general(q_heads[hh], kt, NT, preferred_element_type=F32)
            s = s - c_ref[pl.ds(2 * hp + hh, 1), pl.ds(ks, tk)]
            if diag is not None:
                s = jnp.where(col + diag * tk <= row, s, -jnp.inf)
            m_new = jnp.maximum(m, jnp.max(s, axis=-1, keepdims=True))
            alpha = jnp.exp(m - m_new)
            p = jnp.exp(s - m_new)
            l = alpha * l + jnp.sum(p, axis=-1, keepdims=True)
            acc = alpha * acc + jnp.dot(p.astype(BF16), vt, preferred_element_type=F32)
            new.append((m_new, l, acc))
        return tuple(new)

    init = (jnp.full((tq, 1), -jnp.inf, F32), jnp.zeros((tq, 1), F32),
            jnp.zeros((tq, LANES), F32))
    carry = lax.fori_loop(0, qi * sub, functools.partial(step, diag=None), (init, init))
    for d in range(sub):
        carry = step(qi * sub + d, carry, d)
    (_, l0, acc0), (_, l1, acc1) = carry
    o_ref[...] = jnp.where(lane < HEAD_DIM, acc0 / l0, acc1 / l1).astype(BF16)


def _prompt_attention(q, kb, vb, c_t, batch, seq):
    tq, tk = ATTN_Q_TILE, ATTN_K_TILE
    nq = seq // tq
    return pl.pallas_call(
        functools.partial(_pattn_kernel, tq=tq, tk=tk),
        grid=(batch, D_ATTN // LANES, nq),
        in_specs=[
            pl.BlockSpec((tq, LANES), lambda b, h, i: (b * nq + i, h)),
            pl.BlockSpec((seq, LANES), lambda b, h, i: (b, h)),
            pl.BlockSpec((seq, LANES), lambda b, h, i: (b, h)),
            pl.BlockSpec((N_HEADS, seq), lambda b, h, i: (0, b)),
        ],
        out_specs=pl.BlockSpec((tq, LANES), lambda b, h, i: (b * nq + i, h)),
        out_shape=jax.ShapeDtypeStruct((batch * seq, D_ATTN), BF16),
        compiler_params=pltpu.CompilerParams(
            dimension_semantics=("arbitrary", "arbitrary", "arbitrary"),
            vmem_limit_bytes=VMEM_LIMIT),
        name="pattn",
    )(q, kb, vb, c_t)


def _sattn_kernel(pt_ref, q_ref, kn_ref, vn_ref, lfn_ref, triu_ref, pagesel_ref, *rest,
                  n_pages, page, t_new):
    del pt_ref
    k_refs = rest[:n_pages]
    v_refs = rest[n_pages:2 * n_pages]
    f_refs = rest[2 * n_pages:3 * n_pages]
    o_ref = rest[3 * n_pages]
    nrow = t_new * N_HEADS
    q = q_ref[0]
    qrep = jnp.concatenate([jnp.broadcast_to(q[t:t + 1], (N_HEADS, D_ATTN)) for t in range(t_new)],
                           axis=0)
    rowi = lax.broadcasted_iota(I32, (nrow, D_ATTN), 0)
    lanei = lax.broadcasted_iota(I32, (nrow, D_ATTN), 1)
    head_mask = (lanei // HEAD_DIM) == (rowi % N_HEADS)
    qexp = jnp.where(head_mask, qrep, 0.0).astype(BF16)
    qexp_f = qexp.astype(F32)

    lf = jnp.concatenate([f_refs[p][0] for p in range(n_pages)], axis=0)
    c_in = _dot_sel(lf, triu_ref[...])
    tot = jnp.broadcast_to(c_in[:, page - 1:page], (n_pages * N_HEADS, page))
    c_all = c_in + _sel_dot(pagesel_ref[...], tot)
    c_tot = c_all[(n_pages - 1) * N_HEADS:, page - 1:page]

    logits = []
    for p in range(n_pages):
        kp = k_refs[p][0].astype(BF16)
        s = lax.dot_general(qexp, kp, NT, preferred_element_type=F32)
        cp = c_all[p * N_HEADS:(p + 1) * N_HEADS]
        logits.append(s - jnp.concatenate([cp] * t_new, axis=0))
    m = logits[0]
    for s in logits[1:]:
        m = jnp.maximum(m, s)
    m = jnp.max(m, axis=-1, keepdims=True)

    kn = kn_ref[0].astype(BF16).astype(F32)
    vn = vn_ref[0].astype(BF16).astype(F32)
    lfn = lfn_ref[0]
    trow = lax.broadcasted_iota(I32, (nrow, 1), 0) // N_HEADS
    new_logits = []
    cn = c_tot
    for j in range(t_new):
        cn = cn + lfn[:, j:j + 1]
        sj = jnp.sum(qexp_f * kn[j:j + 1], axis=-1, keepdims=True)
        sj = sj - jnp.concatenate([cn] * t_new, axis=0)
        sj = jnp.where(trow >= j, sj, -jnp.inf)
        new_logits.append(sj)
        m = jnp.maximum(m, sj)

    l = jnp.zeros((nrow, 1), F32)
    acc = jnp.zeros((nrow, D_ATTN), F32)
    for p in range(n_pages):
        pr = jnp.exp(logits[p] - m)
        l = l + jnp.sum(pr, axis=-1, keepdims=True)
        acc = acc + jnp.dot(pr.astype(BF16), v_refs[p][0].astype(BF16), preferred_element_type=F32)
    for j in range(t_new):
        pj = jnp.exp(new_logits[j] - m)
        l = l + pj
        acc = acc + pj.astype(BF16).astype(F32) * vn[j:j + 1]
    acc = jnp.where(head_mask, acc / l, 0.0)
    o_ref[0] = jnp.concatenate(
        [jnp.sum(acc[t * N_HEADS:(t + 1) * N_HEADS], axis=0, keepdims=True) for t in range(t_new)], axis=0)


def _sample_attention(page_table, q_s, k_s, v_s, lf_s, cache_k, cache_v, cache_lft):
    n_seq, n_pages = page_table.shape
    page = cache_k.shape[1]
    t_new = q_s.shape[1]
    triu = jnp.triu(jnp.ones((page, page), F32)).astype(BF16)
    r = jnp.arange(n_pages * N_HEADS)
    pagesel = ((r[:, None] % N_HEADS == r[None, :] % N_HEADS)
               & (r[None, :] // N_HEADS < r[:, None] // N_HEADS)).astype(BF16)

    def page_spec(p, shape):
        return pl.BlockSpec((1,) + shape, lambda n, pt, p=p: (pt[n * n_pages + p], 0, 0))

    seq_spec = lambda a: pl.BlockSpec((1,) + a.shape[1:], lambda n, pt: (n, 0, 0))
    const = lambda a: pl.BlockSpec(a.shape, lambda n, pt: (0, 0))
    grid_spec = pltpu.PrefetchScalarGridSpec(
        num_scalar_prefetch=1,
        grid=(n_seq,),
        in_specs=[seq_spec(q_s), seq_spec(k_s), seq_spec(v_s), seq_spec(lf_s), const(triu), const(pagesel)]
        + [page_spec(p, (page, D_ATTN)) for p in range(n_pages)]
        + [page_spec(p, (page, D_ATTN)) for p in range(n_pages)]
        + [page_spec(p, (N_HEADS, page)) for p in range(n_pages)],
        out_specs=pl.BlockSpec((1, t_new, D_ATTN), lambda n, pt: (n, 0, 0)),
    )
    return pl.pallas_call(
        functools.partial(_sattn_kernel, n_pages=n_pages, page=page, t_new=t_new),
        grid_spec=grid_spec,
        out_shape=jax.ShapeDtypeStruct((n_seq, t_new, D_ATTN), F32),
        compiler_params=pltpu.CompilerParams(dimension_semantics=("arbitrary",),
                                             vmem_limit_bytes=VMEM_LIMIT),
        name="sattn",
    )(page_table.reshape(-1), q_s, k_s, v_s, lf_s, triu, pagesel,
      *([cache_k] * n_pages), *([cache_v] * n_pages), *([cache_lft] * n_pages))


def _sattn_native_kernel(pt_ref, q_ref, kn_ref, vn_ref, lfn_ref, scan_ref, last_ref, lower_ref, *rest,
                         n_pages, page, t_new):
    del pt_ref
    k_refs = rest[:n_pages]
    v_refs = rest[n_pages:2 * n_pages]
    f_refs = rest[2 * n_pages:3 * n_pages]
    o_ref = rest[3 * n_pages]
    nrow = t_new * N_HEADS
    width = page * N_HEADS
    chunks = width // LANES
    q = q_ref[0].astype(BF16)
    rowi = lax.broadcasted_iota(I32, (nrow, width), 0)
    lanei = lax.broadcasted_iota(I32, (nrow, width), 1)
    same_head = (lanei % N_HEADS) == (rowi % N_HEADS)

    lf = jnp.concatenate([f_refs[p][0] for p in range(n_pages)], axis=0)
    local = _dot_sel(lf, scan_ref[...])
    tot = _dot_sel(local, last_ref[...])
    before = _sel_dot(lower_ref[...], tot)
    c_all = local + before
    c_tot = before[n_pages * chunks - 1:] + tot[n_pages * chunks - 1:]

    logits = []
    for p in range(n_pages):
        kp = k_refs[p][...].reshape(width, HEAD_DIM).astype(BF16)
        s = lax.dot_general(q, kp, NT, preferred_element_type=F32)
        sc = [s[:, j * LANES:(j + 1) * LANES] - c_all[p * chunks + j:p * chunks + j + 1]
              for j in range(chunks)]
        logits.append(jnp.where(same_head, jnp.concatenate(sc, axis=1), -jnp.inf))
    m = logits[0]
    for s in logits[1:]:
        m = jnp.maximum(m, s)
    m = jnp.max(m, axis=-1, keepdims=True)

    pad = jnp.zeros((LANES - nrow, HEAD_DIM), F32)
    kn = jnp.concatenate([kn_ref[0], pad], axis=0).astype(BF16)
    vn = jnp.concatenate([vn_ref[0], pad], axis=0).astype(BF16)
    cn = c_tot + _dot_sel(lfn_ref[0], scan_ref[...])[0:1]
    sn = lax.dot_general(q, kn, NT, preferred_element_type=F32) - cn
    rown = lax.broadcasted_iota(I32, (nrow, LANES), 0)
    lanen = lax.broadcasted_iota(I32, (nrow, LANES), 1)
    visible = jnp.logical_and(lanen < nrow, lanen // N_HEADS <= rown // N_HEADS)
    visible = jnp.logical_and(visible, (lanen % N_HEADS) == (rown % N_HEADS))
    sn = jnp.where(visible, sn, -jnp.inf)
    m = jnp.maximum(m, jnp.max(sn, axis=-1, keepdims=True))

    pn = jnp.exp(sn - m)
    l = jnp.sum(pn, axis=-1, keepdims=True)
    acc = jnp.dot(pn.astype(BF16), vn, preferred_element_type=F32)
    for p in range(n_pages):
        pr = jnp.exp(logits[p] - m)
        l = l + jnp.sum(pr, axis=-1, keepdims=True)
        vp = v_refs[p][...].reshape(width, HEAD_DIM).astype(BF16)
        acc = acc + jnp.dot(pr.astype(BF16), vp, preferred_element_type=F32)
    o_ref[0] = acc / l


def _sample_attention_native(page_table, q_s, k_s, v_s, lf_s, cache_k, cache_v, cache_lf):
    n_seq, n_pages = page_table.shape
    page = cache_k.shape[2]
    width = page * N_HEADS
    chunks = width // LANES
    nrow = q_s.shape[1]
    kv_spec = lambda p: pl.BlockSpec((None, None, page, N_HEADS, HEAD_DIM),
                                     lambda n, pt, p=p: (0, pt[n * n_pages + p], 0, 0, 0))
    r = jnp.arange(LANES)
    same = r[:, None] % N_HEADS == r[None, :] % N_HEADS
    scan = (same & (r[:, None] // N_HEADS <= r[None, :] // N_HEADS)).astype(BF16)
    last = (same & (r[:, None] // N_HEADS == LANES // N_HEADS - 1)).astype(BF16)
    lower = jnp.tril(jnp.ones((n_pages * chunks, n_pages * chunks), F32), -1).astype(BF16)

    def page_spec(p, shape):
        return pl.BlockSpec((1,) + shape, lambda n, pt, p=p: (pt[n * n_pages + p], 0, 0))

    seq_spec = lambda a: pl.BlockSpec((1,) + a.shape[1:], lambda n, pt: (n, 0, 0))
    const = lambda a: pl.BlockSpec(a.shape, lambda n, pt: (0, 0))
    grid_spec = pltpu.PrefetchScalarGridSpec(
        num_scalar_prefetch=1,
        grid=(n_seq,),
        in_specs=[seq_spec(q_s), seq_spec(k_s), seq_spec(v_s), seq_spec(lf_s), const(scan), const(last),
                  const(lower)]
        + [kv_spec(p) for p in range(n_pages)]
        + [kv_spec(p) for p in range(n_pages)]
        + [page_spec(p, (chunks, LANES)) for p in range(n_pages)],
        out_specs=pl.BlockSpec((1, nrow, HEAD_DIM), lambda n, pt: (n, 0, 0)),
    )
    return pl.pallas_call(
        functools.partial(_sattn_native_kernel, n_pages=n_pages, page=page, t_new=nrow // N_HEADS),
        grid_spec=grid_spec,
        out_shape=jax.ShapeDtypeStruct((n_seq, nrow, HEAD_DIM), F32),
        compiler_params=pltpu.CompilerParams(dimension_semantics=("arbitrary",),
                                             vmem_limit_bytes=VMEM_LIMIT),
        name="sattn",
    )(page_table.reshape(-1), q_s, k_s, v_s, lf_s, scan, last, lower,
      *([cache_k] * n_pages), *([cache_v] * n_pages), *([cache_lf] * n_pages))


def _mixer_kernel(x_ref, yap_ref, yas_ref, tap0_ref, tap1_ref, lng_ref, lnb_ref, wc_ref, wg_ref,
                  cw_ref, wbc_ref, wba_ref, wo_ref, l1g_ref, l1b_ref, wr_ref, br_ref, tril_ref,
                  x1_ref, route_ref, cnt_ref, ulast_ref, us_ref,
                  ext_ref, cnt_acc_ref, *, tiles_per_seq, n_prompt_tiles, t_new, alpha):
    i = pl.program_id(0)
    tm = x_ref.shape[0]
    d_conv = us_ref.shape[1]
    is_sample = i >= n_prompt_tiles
    xn = _ln(x_ref[...], lng_ref[...], lnb_ref[...])
    xb = xn.astype(BF16)

    zc = jnp.dot(xb, wc_ref[...], preferred_element_type=F32)
    u = zc[:, d_conv:2 * d_conv] * zc[:, 2 * d_conv:]

    @pl.when(i % tiles_per_seq == 0)
    def _():
        ext_ref[0:SUBLANES, :] = jnp.zeros((SUBLANES, d_conv), F32)

    ext_ref[SUBLANES:, :] = u
    prev1 = ext_ref[SUBLANES - 1:SUBLANES - 1 + tm, :]
    prev2 = ext_ref[SUBLANES - 2:SUBLANES - 2 + tm, :]
    ext_ref[0:SUBLANES, :] = u[tm - SUBLANES:]
    t_in_seq = lax.broadcasted_iota(I32, (tm, 1), 0) % t_new
    keep1 = jnp.logical_or(jnp.logical_not(is_sample), t_in_seq >= 1)
    keep2 = jnp.logical_or(jnp.logical_not(is_sample), t_in_seq >= 2)
    prev1 = jnp.where(keep1, prev1, tap1_ref[...])
    prev2 = jnp.where(keep2, prev2, tap0_ref[...])
    cw = cw_ref[...]
    y_conv = zc[:, :d_conv] * (prev2 * cw[0:1] + prev1 * cw[1:2] + u * cw[2:3])
    ulast_ref[0] = u[tm - SUBLANES:]
    us_ref[...] = u

    ya = jnp.where(is_sample, yas_ref[...].astype(BF16), yap_ref[...])
    bc = jnp.dot(y_conv.astype(BF16), wbc_ref[...], preferred_element_type=F32)
    ba = jnp.dot(ya, wba_ref[...], preferred_element_type=F32)
    g = jnp.dot(xb, wg_ref[...], preferred_element_type=F32)
    d_model = bc.shape[1]
    merged = _sigmoid(g[:, :d_model]) * bc + _sigmoid(g[:, d_model:]) * ba
    mix = jnp.dot(merged.astype(BF16), wo_ref[...], preferred_element_type=F32)
    x1 = _ln(alpha * xn + mix, l1g_ref[...], l1b_ref[...])
    x1_ref[...] = x1

    logits = jnp.dot(x1, wr_ref[...], preferred_element_type=F32,
                     precision=lax.Precision.HIGHEST) + br_ref[...]
    lane = lax.broadcasted_iota(I32, (tm, LANES), 1)
    lane_f = lane.astype(F32)
    logits = jnp.where(lane < N_EXPERTS, logits, -jnp.inf)
    vals, hots = [], []
    for _ in range(TOP_K):
        mx = jnp.max(logits, axis=-1, keepdims=True)
        idx = jnp.min(jnp.where(logits == mx, lane_f, float(LANES)), axis=-1, keepdims=True)
        hot = lane_f == idx
        logits = jnp.where(hot, -jnp.inf, logits)
        vals.append(mx)
        hots.append(hot)
    exps = [jnp.exp(v - vals[0]) for v in vals]
    denom = exps[0]
    for e in exps[1:]:
        denom = denom + e

    @pl.when(i == 0)
    def _():
        cnt_acc_ref[...] = jnp.zeros_like(cnt_acc_ref)

    chosen = hots[0]
    for h in hots[1:]:
        chosen = jnp.logical_or(chosen, h)
    chosen_f = jnp.where(chosen, 1.0, 0.0)
    before = jnp.dot(tril_ref[...], chosen_f.astype(BF16), preferred_element_type=F32) + cnt_acc_ref[0:1, :]
    route = jnp.zeros((tm, LANES), F32)
    for kk in range(TOP_K):
        idx_f = jnp.sum(jnp.where(hots[kk], lane_f, 0.0), axis=-1, keepdims=True)
        rank = jnp.sum(jnp.where(hots[kk], before, 0.0), axis=-1, keepdims=True)
        route = jnp.where(lane == kk, idx_f, route)
        route = jnp.where(lane == TOP_K + kk, rank, route)
        route = jnp.where(lane == 2 * TOP_K + kk, exps[kk] / denom, route)
    route_ref[...] = route
    cnt = cnt_acc_ref[0:1, :] + jnp.sum(chosen_f, axis=0, keepdims=True)
    cnt_acc_ref[...] = jnp.broadcast_to(cnt, cnt_acc_ref.shape)
    cnt_ref[...] = jnp.broadcast_to(cnt, cnt_ref.shape)


def _mixer(x, ya_p, ya_s, tap0, tap1, ln_g, ln_b, w_c, w_g, conv_w, w_bc, w_ba, w_o, l1g, l1b,
           w_r, b_r, seq, n_prompt_rows, t_new, alpha):
    rows, d = x.shape
    tm = ROW_TILE
    n_prompt_tiles = n_prompt_rows // tm
    n_tiles = rows // tm
    d_conv = w_bc.shape[0]
    tril = jnp.tril(jnp.ones((tm, tm), F32), -1).astype(BF16)
    row_spec = lambda w: pl.BlockSpec((tm, w), lambda i: (i, 0))
    prompt_spec = lambda w: pl.BlockSpec((tm, w), lambda i: (jnp.minimum(i, n_prompt_tiles - 1), 0))
    sample_spec = lambda w: pl.BlockSpec((tm, w), lambda i: (jnp.maximum(i - n_prompt_tiles, 0), 0))
    full = lambda a: pl.BlockSpec(a.shape, lambda i: (0,) * a.ndim)
    weights = [ln_g, ln_b, w_c, w_g, conv_w, w_bc, w_ba, w_o, l1g, l1b, w_r, b_r, tril]
    n_sample_rows = rows - n_prompt_rows
    return pl.pallas_call(
        functools.partial(_mixer_kernel, tiles_per_seq=seq // tm, n_prompt_tiles=n_prompt_tiles,
                          t_new=t_new, alpha=alpha),
        grid=(n_tiles,),
        in_specs=[row_spec(d), prompt_spec(D_ATTN), sample_spec(D_ATTN), sample_spec(d_conv),
                  sample_spec(d_conv)] + [full(w) for w in weights],
        out_specs=[row_spec(d), row_spec(LANES), pl.BlockSpec((SUBLANES, LANES), lambda i: (0, 0)),
                   pl.BlockSpec((1, SUBLANES, d_conv), lambda i: (i, 0, 0)), sample_spec(d_conv)],
        out_shape=[jax.ShapeDtypeStruct((rows, d), F32), jax.ShapeDtypeStruct((rows, LANES), F32),
                   jax.ShapeDtypeStruct((SUBLANES, LANES), F32),
                   jax.ShapeDtypeStruct((n_tiles, SUBLANES, d_conv), F32),
                   jax.ShapeDtypeStruct((n_sample_rows, d_conv), F32)],
        scratch_shapes=[pltpu.VMEM((tm + SUBLANES, d_conv), F32), pltpu.VMEM((SUBLANES, LANES), F32)],
        compiler_params=pltpu.CompilerParams(dimension_semantics=("arbitrary",),
                                             vmem_limit_bytes=VMEM_LIMIT),
        name="mixer",
    )(x, ya_p, ya_s, tap0, tap1, *weights)


def _dispatch_kernel(meta_ref, dest_ref, x_ref, xs_ref, zero_ref, sem_ref, *, tile, block, n_blocks):
    i = pl.program_id(0)

    def row_copy(j, kk):
        return pltpu.make_async_copy(x_ref.at[pl.ds(j, 1), :],
                                     xs_ref.at[pl.ds(dest_ref[j * TOP_K + kk], 1), :], sem_ref.at[0])

    def zero_copy(start):
        return pltpu.make_async_copy(zero_ref, xs_ref.at[pl.ds(pl.multiple_of(start, block), block), :],
                                     sem_ref.at[1])

    def pad_copy(e):
        return zero_copy(meta_ref[e] - block)

    @pl.when(i == 0)
    def _():
        zero_ref[...] = jnp.zeros_like(zero_ref)
        n_used = meta_ref[2 * N_EXPERTS]
        for e in range(N_EXPERTS):
            @pl.when(meta_ref[N_EXPERTS + e] > 0)
            def _():
                pad_copy(e).start()

        def tail_start(b, carry):
            zero_copy(b * block).start()
            return carry

        def tail_wait(b, carry):
            zero_copy(b * block).wait()
            return carry

        lax.fori_loop(n_used, n_blocks, tail_start, 0)
        for e in range(N_EXPERTS):
            @pl.when(meta_ref[N_EXPERTS + e] > 0)
            def _():
                pad_copy(e).wait()
        lax.fori_loop(n_used, n_blocks, tail_wait, 0)

    def issue(j, carry):
        for kk in range(TOP_K):
            row_copy(j, kk).start()
        return carry

    def drain(j, carry):
        for kk in range(TOP_K):
            row_copy(j, kk).wait()
        return carry

    lax.fori_loop(0, tile, issue, 0)
    lax.fori_loop(0, tile, drain, 0)


def _dispatch(meta, dest, x1, n_rows_padded):
    rows, d = x1.shape
    tile = MOE_TILE
    grid_spec = pltpu.PrefetchScalarGridSpec(
        num_scalar_prefetch=1,
        grid=(rows // tile,),
        in_specs=[pl.BlockSpec((tile * TOP_K,), lambda i, m: (i,), memory_space=pltpu.SMEM),
                  pl.BlockSpec((tile, d), lambda i, m: (i, 0))],
        out_specs=pl.BlockSpec(memory_space=pl.ANY),
        scratch_shapes=[pltpu.VMEM((MOE_BLOCK, d), F32), pltpu.SemaphoreType.DMA((2,))],
    )
    return pl.pallas_call(
        functools.partial(_dispatch_kernel, tile=tile, block=MOE_BLOCK,
                          n_blocks=n_rows_padded // MOE_BLOCK),
        grid_spec=grid_spec,
        out_shape=jax.ShapeDtypeStruct((n_rows_padded, d), F32),
        compiler_params=pltpu.CompilerParams(dimension_semantics=("arbitrary",),
                                             vmem_limit_bytes=VMEM_LIMIT),
        name="dispatch",
    )(meta, dest, x1)


def _experts_kernel(be_ref, nb_ref, xs_ref, wg_ref, bg_ref, wu_ref, bu_ref, wd_ref, bd_ref, o_ref,
                    wgb_ref, wub_ref, wdb_ref):
    b = pl.program_id(0)
    prev = be_ref[jnp.maximum(b - 1, 0)]

    @pl.when(jnp.logical_or(b == 0, be_ref[b] != prev))
    def _():
        wgb_ref[...] = wg_ref[0].astype(BF16)
        wub_ref[...] = wu_ref[0].astype(BF16)
        wdb_ref[...] = wd_ref[0].astype(BF16)

    @pl.when(b < nb_ref[0])
    def _():
        xb = xs_ref[...].astype(BF16)
        g = jnp.dot(xb, wgb_ref[...], preferred_element_type=F32) + bg_ref[0]
        u = jnp.dot(xb, wub_ref[...], preferred_element_type=F32) + bu_ref[0]
        g = jnp.minimum(g, SWIGLU_LIMIT)
        u = jnp.clip(u, -SWIGLU_LIMIT, SWIGLU_LIMIT)
        h = (u + 1.0) * (g * _sigmoid(SWIGLU_ALPHA * g))
        o_ref[...] = jnp.dot(h.astype(BF16), wdb_ref[...], preferred_element_type=F32) + bd_ref[0]

    @pl.when(b >= nb_ref[0])
    def _():
        o_ref[...] = jnp.zeros_like(o_ref)


def _experts(block_e, n_used, xs, w_gate, b_gate, w_up, b_up, w_down, b_down):
    n_rows, d = xs.shape
    bm = MOE_BLOCK
    d_e = w_gate.shape[2]
    x_spec = pl.BlockSpec((bm, d), lambda b, be, nb: (b, 0))
    w_spec = lambda a: pl.BlockSpec((1,) + a.shape[1:], lambda b, be, nb: (be[b], 0, 0))
    grid_spec = pltpu.PrefetchScalarGridSpec(
        num_scalar_prefetch=2,
        grid=(n_rows // bm,),
        in_specs=[x_spec, w_spec(w_gate), w_spec(b_gate), w_spec(w_up), w_spec(b_up), w_spec(w_down),
                  w_spec(b_down)],
        out_specs=x_spec,
        scratch_shapes=[pltpu.VMEM((d, d_e), BF16), pltpu.VMEM((d, d_e), BF16), pltpu.VMEM((d_e, d), BF16)],
    )
    return pl.pallas_call(
        _experts_kernel,
        grid_spec=grid_spec,
        out_shape=jax.ShapeDtypeStruct((n_rows, d), F32),
        compiler_params=pltpu.CompilerParams(dimension_semantics=("arbitrary",),
                                             vmem_limit_bytes=VMEM_LIMIT),
        name="experts",
    )(block_e, n_used, xs, w_gate, b_gate, w_up, b_up, w_down, b_down)


def _combine_kernel(dest_ref, x1_ref, route_ref, pp_ref, ps_ref, ys_ref, l2g_ref, l2b_ref, wpg_ref, wpp_ref,
                    y_ref, gbuf_ref, sem_ref, *, tile, n_prompt_tiles, alpha):
    i = pl.program_id(0)

    def row_copy(j, kk):
        return pltpu.make_async_copy(ys_ref.at[pl.ds(dest_ref[j * TOP_K + kk], 1), :],
                                     gbuf_ref.at[kk, pl.ds(j, 1), :], sem_ref.at[0])

    def issue(j, carry):
        for kk in range(TOP_K):
            row_copy(j, kk).start()
        return carry

    def drain(j, carry):
        for kk in range(TOP_K):
            row_copy(j, kk).wait()
        return carry

    lax.fori_loop(0, tile, issue, 0)
    lax.fori_loop(0, tile, drain, 0)

    route = route_ref[...]
    ffn = route[:, 2 * TOP_K:2 * TOP_K + 1] * gbuf_ref[0]
    for kk in range(1, TOP_K):
        ffn = ffn + route[:, 2 * TOP_K + kk:2 * TOP_K + kk + 1] * gbuf_ref[kk]
    x2 = _ln(alpha * x1_ref[...] + ffn, l2g_ref[...], l2b_ref[...])
    p = jnp.where(i >= n_prompt_tiles, ps_ref[...], pp_ref[...]).astype(BF16)
    gate = _sigmoid(jnp.dot(x2.astype(BF16), wpg_ref[...], preferred_element_type=F32))
    y_ref[...] = x2 + gate * jnp.dot(p, wpp_ref[...], preferred_element_type=F32)


def _combine(dest, x1, route, p_p, p_s, ys, l2g, l2b, w_pg, w_pp, n_prompt_rows, alpha):
    rows, d = x1.shape
    tile = MOE_TILE
    n_prompt_tiles = n_prompt_rows // tile
    d_ple = p_p.shape[1]
    row_spec = lambda w: pl.BlockSpec((tile, w), lambda i: (i, 0))
    full = lambda a: pl.BlockSpec(a.shape, lambda i: (0,) * a.ndim)
    return pl.pallas_call(
        functools.partial(_combine_kernel, tile=tile, n_prompt_tiles=n_prompt_tiles, alpha=alpha),
        grid=(rows // tile,),
        in_specs=[pl.BlockSpec((tile * TOP_K,), lambda i: (i,), memory_space=pltpu.SMEM),
                  row_spec(d), row_spec(LANES),
                  pl.BlockSpec((tile, d_ple), lambda i: (jnp.minimum(i, n_prompt_tiles - 1), 0)),
                  pl.BlockSpec((tile, d_ple), lambda i: (jnp.maximum(i - n_prompt_tiles, 0), 0)),
                  pl.BlockSpec(memory_space=pl.ANY), full(l2g), full(l2b), full(w_pg), full(w_pp)],
        out_specs=row_spec(d),
        out_shape=jax.ShapeDtypeStruct((rows, d), F32),
        scratch_shapes=[pltpu.VMEM((TOP_K, tile, d), F32), pltpu.SemaphoreType.DMA((1,))],
        compiler_params=pltpu.CompilerParams(dimension_semantics=("arbitrary",),
                                             vmem_limit_bytes=VMEM_LIMIT),
        name="combine",
    )(dest, x1, route, p_p, p_s, ys, l2g, l2b, w_pg, w_pp)


def kernel(x_prompt, x_sample, cache_k, cache_v, cache_logf, state_conv, page_table, p_prompt, p_sample,
           ln_in_g, ln_in_b, w_in, b_f, conv_w, w_br_conv, w_br_attn, w_o, ln1_g, ln1_b, w_router,
           b_router, w_gate, b_gate, w_up, b_up, w_down, b_down, ln2_g, ln2_b, w_ple_gate, w_ple_proj):
    depth = w_in.shape[0]
    assert depth == 1, "single-layer step only"
    batch, seq, d = x_prompt.shape
    n_seq, t_new, _ = x_sample.shape
    d_conv = conv_w.shape[2]
    n_pool, page = cache_k.shape[1], cache_k.shape[2]
    alpha = (2 * depth) ** 0.25
    n_prompt_rows = batch * seq
    n_sample_rows = n_seq * t_new
    rows = n_prompt_rows + n_sample_rows
    assert seq % ROW_TILE == 0 and n_sample_rows % ROW_TILE == 0 and seq % ATTN_Q_TILE == 0
    assert ATTN_Q_TILE % ATTN_K_TILE == 0
    assert t_new >= CONV_TAPS - 1

    row2 = lambda a: a.reshape(1, -1)
    x = jnp.concatenate([x_prompt.reshape(n_prompt_rows, d), x_sample.reshape(n_sample_rows, d)], axis=0)

    w = w_in[0]
    o_qkv = 3 * d_conv
    o_f = o_qkv + 3 * D_ATTN
    o_g = o_f + N_HEADS
    w_c = w[:, :o_qkv].astype(BF16)
    w_qkv = w[:, o_qkv:o_f].astype(BF16)
    w_ft = jnp.zeros((2 * SUBLANES, d), BF16).at[:N_HEADS].set(w[:, o_f:o_g].T.astype(BF16))
    w_g = w[:, o_g:].astype(BF16)

    q, k, v, kb, vb, lf_t, c_t = _qkv(x, row2(ln_in_g), row2(ln_in_b), w_qkv, w_ft,
                                      b_f[0].reshape(N_HEADS, 1), seq)

    ya_p = _prompt_attention(q, kb, vb, c_t, batch, seq)

    sl = lambda a: a[n_prompt_rows:].reshape(n_seq, t_new * N_HEADS, HEAD_DIM)
    lf_new = lf_t[:, n_prompt_rows:].T.reshape(n_seq, 1, t_new * N_HEADS)
    lf_s = jnp.pad(lf_new, ((0, 0), (0, SUBLANES - 1), (0, LANES - t_new * N_HEADS)))
    ya_s = _sample_attention_native(page_table, sl(q), sl(k), sl(v), lf_s,
                                    cache_k, cache_v,
                                    cache_logf[0].reshape(n_pool, page * N_HEADS // LANES, LANES))

    st = state_conv[0]
    zeros = jnp.zeros((n_seq, t_new - 2, d_conv), F32)
    tap0 = jnp.concatenate([st, zeros], axis=1).reshape(n_sample_rows, d_conv)
    tap1 = jnp.concatenate([st[:, 1:], jnp.zeros((n_seq, t_new - 1, d_conv), F32)],
                           axis=1).reshape(n_sample_rows, d_conv)

    w_r = jnp.zeros((d, LANES), F32).at[:, :N_EXPERTS].set(w_router[0])
    b_r = jnp.zeros((1, LANES), F32).at[:, :N_EXPERTS].set(b_router[0])
    conv_w8 = jnp.zeros((SUBLANES, d_conv), F32).at[:CONV_TAPS].set(conv_w[0])
    x1, route, counts, u_last, u_s = _mixer(
        x, ya_p, ya_s.reshape(n_sample_rows, D_ATTN), tap0, tap1, row2(ln_in_g), row2(ln_in_b), w_c, w_g,
        conv_w8, w_br_conv[0].astype(BF16), w_br_attn[0].astype(BF16), w_o[0].astype(BF16),
        row2(ln1_g[0]), row2(ln1_b[0]), w_r, b_r, seq, n_prompt_rows, t_new, alpha)

    bm = MOE_BLOCK
    n_assign = rows * TOP_K
    n_blocks = -(-n_assign // bm) + N_EXPERTS
    cnt = counts[0, :N_EXPERTS].astype(I32)
    padded = (cnt + bm - 1) // bm * bm
    pad_end = jnp.cumsum(padded)
    pad_start = pad_end - padded
    idx = route[:, :TOP_K].astype(I32)
    rank = route[:, TOP_K:2 * TOP_K].astype(I32)
    dest = (pad_start[idx] + rank).reshape(-1)
    n_used = (pad_end[-1] // bm).astype(I32).reshape(1)
    blk = jnp.minimum(jnp.arange(n_blocks, dtype=I32), n_used[0] - 1) * bm
    block_e = jnp.minimum(jnp.sum(blk[:, None] >= pad_end[None, :], axis=1), N_EXPERTS - 1).astype(I32)
    meta = jnp.concatenate([pad_end, padded, n_used]).astype(I32)

    xs = _dispatch(meta, dest, x1, n_blocks * bm)
    as3 = lambda a: a[0].reshape(N_EXPERTS, 1, -1)
    ys = _experts(block_e, n_used, xs, w_gate[0], as3(b_gate), w_up[0], as3(b_up), w_down[0], as3(b_down))
    y = _combine(dest, x1, route, p_prompt[0].reshape(n_prompt_rows, -1),
                 p_sample[0].reshape(n_sample_rows, -1), ys, row2(ln2_g[0]), row2(ln2_b[0]),
                 w_ple_gate[0].astype(BF16), w_ple_proj[0].astype(BF16), n_prompt_rows, alpha)

    tiles_per_seq = seq // ROW_TILE
    conv_p = u_last[tiles_per_seq - 1::tiles_per_seq][:batch, SUBLANES - 2:]
    conv_s = u_s.reshape(n_seq, t_new, d_conv)[:, t_new - 2:]
    pr = lambda a, *s: a[:n_prompt_rows].reshape((1, batch, seq) + s)
    sr = lambda a, *s: a[n_prompt_rows:].reshape((1, n_seq, t_new) + s)
    lf = lf_t.T
    return (y[:n_prompt_rows].reshape(batch, seq, d), y[n_prompt_rows:].reshape(n_seq, t_new, d),
            pr(k, N_HEADS, HEAD_DIM), pr(v, N_HEADS, HEAD_DIM), pr(lf, N_HEADS), conv_p[None],
            sr(k, N_HEADS, HEAD_DIM), sr(v, N_HEADS, HEAD_DIM), sr(lf, N_HEADS), conv_s[None])
```
